```python
import jax, jax.numpy as jnp
from jax import lax
import numpy as np

D_MODEL = 1024
BATCH = 4
SEQ = 4096
DEPTH = 1
DEC_BATCH = 32
DEC_SEQ = 1
PAST_LEN = 16384
PAGE_SIZE = 128

N_HEADS = 8
HEAD_DIM = 64
ATT_DIM = N_HEADS * HEAD_DIM
CONV_DIM = D_MODEL - ATT_DIM
CONV_GROUPS = 8
CONV_WIDTH = 3
Q_BLOCK = 128
EPS = 1e-6
SB_BIAS_INIT = -8.0
PROJ_WIDTHS = (CONV_DIM, CONV_DIM, CONV_DIM, CONV_DIM, ATT_DIM, ATT_DIM, ATT_DIM, ATT_DIM)
PROJ_DIM = 4 * CONV_DIM + 4 * ATT_DIM

kernel_name = "hymba_shortconv_stickbreaking_decode_step"


def _rmsnorm(x, g):
    xf = x.astype(jnp.float32)
    y = xf * lax.rsqrt(jnp.mean(xf * xf, axis=-1, keepdims=True) + EPS)
    return (y * g.astype(jnp.float32)).astype(x.dtype)


def _split_points():
    pts, acc = [], 0
    for w in PROJ_WIDTHS[:-1]:
        acc += w
        pts.append(acc)
    return pts


def _stick_breaking(q, k, v, b_sb, q_pos, k_pos):
    z = jnp.einsum('bqhd,bkhd->bhqk', q, k).astype(jnp.float32) * (HEAD_DIM ** -0.5)
    z = z + b_sb.astype(jnp.float32)[None, :, None, None]
    mask = k_pos[None, :] < q_pos[:, None]
    log_keep = jnp.where(mask, jax.nn.log_sigmoid(-z), 0.0)
    after = lax.cumsum(log_keep, axis=3, reverse=True) - log_keep
    a = jnp.where(mask, jnp.exp(jax.nn.log_sigmoid(z) + after), 0.0)
    return jnp.einsum('bhqk,bkhd->bqhd', a.astype(v.dtype), v)


def _prompt_attend(q, k, v, b_sb):
    B, S, H, Dh = q.shape
    nb = S // Q_BLOCK
    qb = q.reshape(B, nb, Q_BLOCK, H, Dh).transpose(1, 0, 2, 3, 4)
    k_pos = jnp.arange(S, dtype=jnp.int32)

    def block(args):
        q_blk, i = args
        q_pos = i * Q_BLOCK + jnp.arange(Q_BLOCK, dtype=jnp.int32)
        return _stick_breaking(q_blk, k, v, b_sb, q_pos, k_pos)

    o = lax.map(block, (qb, jnp.arange(nb, dtype=jnp.int32)))
    return o.transpose(1, 0, 2, 3, 4).reshape(B, S, H, Dh)


def _sample_attend(q, k, v, b_sb, pool_k, pool_v, page_table):
    Bd, L, H, Dh = q.shape
    past = page_table.shape[1] * pool_k.shape[1]
    past_k = pool_k[page_table].reshape(Bd, past, H, Dh)
    past_v = pool_v[page_table].reshape(Bd, past, H, Dh)
    kk = jnp.concatenate([past_k, k], axis=1)
    vv = jnp.concatenate([past_v, v], axis=1)
    q_pos = past + jnp.arange(L, dtype=jnp.int32)
    k_pos = jnp.arange(past + L, dtype=jnp.int32)
    return _stick_breaking(q, kk, vv, b_sb, q_pos, k_pos)


def _mixer_layer(x, c, conv_prev, attend, w_ada, b_ada, g_pre, w_in, conv_w,
                 g_conv, b_sb, g_attn, w_out, g_post):
    Bsz, L, _ = x.shape
    shift, scale, gate = jnp.split(jax.nn.silu(c) @ w_ada + b_ada, 3, axis=-1)
    h = _rmsnorm(x, g_pre) * (1 + scale[:, None]) + shift[:, None]
    p = h @ w_in
    hc, bc, cc, gc, q, k, v, ga = jnp.split(p, _split_points(), axis=-1)
    u = cc * hc
    u_pad = jnp.concatenate([conv_prev.astype(u.dtype), u], axis=1)
    conv_y = sum(conv_w[i] * u_pad[:, i:i + L] for i in range(CONV_WIDTH))
    new_conv = u_pad[:, L:]
    ya = _rmsnorm(bc * conv_y, g_conv) * jax.nn.silu(gc)
    q = q.reshape(Bsz, L, N_HEADS, HEAD_DIM)
    k = k.reshape(Bsz, L, N_HEADS, HEAD_DIM)
    v = v.reshape(Bsz, L, N_HEADS, HEAD_DIM)
    o = attend(q, k, v, b_sb).reshape(Bsz, L, ATT_DIM)
    yb = _rmsnorm(o, g_attn) * jax.nn.silu(ga)
    m = jnp.concatenate([ya, yb], axis=-1) @ w_out
    y = x + gate[:, None] * _rmsnorm(m, g_post)
    return y, k, v, new_conv


def setup_inputs(seed: int = 0) -> dict:
    key = jax.random.key(seed)
    ks = jax.random.split(key, 20)
    n_pages = PAST_LEN // PAGE_SIZE
    n_used = DEC_BATCH * n_pages
    n_pool = n_used + max(1, n_used // 4)
    f32 = jnp.float32
    D = D_MODEL
    page_table = jax.random.permutation(ks[0], n_pool)[:n_used].reshape(DEC_BATCH, n_pages).astype(jnp.int32)
    return {
        "x_prompt": jax.random.normal(ks[1], (BATCH, SEQ, D), f32),
        "x_sample": jax.random.normal(ks[2], (DEC_BATCH, DEC_SEQ, D), f32),
        "c_prompt": jax.random.normal(ks[3], (BATCH, D), f32),
        "c_sample": jax.random.normal(ks[4], (DEC_BATCH, D), f32),
        "cache_k": jax.random.normal(ks[5], (DEPTH, n_pool, PAGE_SIZE, N_HEADS, HEAD_DIM), f32),
        "cache_v": jax.random.normal(ks[6], (DEPTH, n_pool, PAGE_SIZE, N_HEADS, HEAD_DIM), f32),
        "state_conv": jax.random.normal(ks[7], (DEPTH, DEC_BATCH, CONV_WIDTH - 1, CONV_DIM), f32),
        "page_table": page_table,
        "w_ada": jax.random.normal(ks[8], (DEPTH, D, 3 * D), f32) * (0.5 * D ** -0.5),
        "b_ada": jax.random.normal(ks[9], (DEPTH, 3 * D), f32) * 0.02,
        "g_pre": 1.0 + 0.01 * jax.random.normal(ks[10], (DEPTH, D), f32),
        "w_in": jax.random.normal(ks[11], (DEPTH, D, PROJ_DIM), f32) * D ** -0.5,
        "conv_w": jax.random.normal(ks[12], (DEPTH, CONV_WIDTH, CONV_DIM), f32) * CONV_WIDTH ** -0.5,
        "g_conv": 1.0 + 0.01 * jax.random.normal(ks[13], (DEPTH, CONV_DIM), f32),
        "b_sb": SB_BIAS_INIT + 0.1 * jax.random.normal(ks[17], (DEPTH, N_HEADS), f32),
        "g_attn": 1.0 + 0.01 * jax.random.normal(ks[14], (DEPTH, ATT_DIM), f32),
        "w_out": jax.random.normal(ks[15], (DEPTH, CONV_DIM + ATT_DIM, D), f32) * (CONV_DIM + ATT_DIM) ** -0.5,
        "g_post": 1.0 + 0.01 * jax.random.normal(ks[16], (DEPTH, D), f32),
    }


def reference(x_prompt, x_sample, c_prompt, c_sample, cache_k, cache_v, state_conv,
              page_table, w_ada, b_ada, g_pre, w_in, conv_w, g_conv, b_sb, g_attn, w_out, g_post):
    xp, xs = x_prompt, x_sample
    kp_l, vp_l, cp_l, ks_l, vs_l, cs_l = [], [], [], [], [], []
    for l in range(DEPTH):
        weights = (w_ada[l], b_ada[l], g_pre[l], w_in[l], conv_w[l], g_conv[l],
                   b_sb[l], g_attn[l], w_out[l], g_post[l])
        conv0 = jnp.zeros((xp.shape[0], CONV_WIDTH - 1, CONV_DIM), xp.dtype)
        xp, kp, vp, cp = _mixer_layer(xp, c_prompt, conv0, _prompt_attend, *weights)
        pool_k, pool_v = cache_k[l], cache_v[l]
        attend_s = lambda q, k, v, b, pk=pool_k, pv=pool_v: _sample_attend(q, k, v, b, pk, pv, page_table)
        xs, ksn, vsn, csn = _mixer_layer(xs, c_sample, state_conv[l], attend_s, *weights)
        kp_l.append(kp); vp_l.append(vp); cp_l.append(cp)
        ks_l.append(ksn); vs_l.append(vsn); cs_l.append(csn)
    return (xp, xs, jnp.stack(kp_l), jnp.stack(vp_l), jnp.stack(cp_l),
            jnp.stack(ks_l), jnp.stack(vs_l), jnp.stack(cs_l))
```

```python
import functools

import jax
import jax.numpy as jnp
from jax import lax
from jax.experimental import pallas as pl
from jax.experimental.pallas import tpu as pltpu

N_HEADS = 8
HEAD_DIM = 64
ATT_DIM = N_HEADS * HEAD_DIM
CONV_WIDTH = 3
EPS = 1e-6

LANES = 128
HEADS_PER_GROUP = LANES // HEAD_DIM
N_GROUPS = ATT_DIM // LANES

PROJ_ROWS = 512
ATT_BLOCK = 256
PAGES_PER_STEP = 8
KEY_CHUNK = 256
VMEM_LIMIT = 52 * 1024 * 1024

F32 = jnp.float32
BF16 = jnp.bfloat16


def _rms(x, g):
    ms = jnp.mean(x * x, axis=-1, keepdims=True)
    return x * lax.rsqrt(ms + EPS) * g


def _softplus(z):
    return jnp.maximum(z, 0.0) + jnp.log(1.0 + jnp.exp(-jnp.abs(z)))


def _split_bf16(x):
    hi = x.astype(BF16)
    lo = (x - hi.astype(F32)).astype(BF16)
    return hi, lo


def _ada_kernel(c_ref, w_ref, b_ref, o_ref):
    c = c_ref[...]
    a = c * jax.nn.sigmoid(c)
    o_ref[...] = jnp.dot(a, w_ref[...], preferred_element_type=F32,
                         precision=lax.Precision.HIGHEST) + b_ref[...]


def _ada(c_all, w_ada, b_ada):
    rows = c_all.shape[0]
    return pl.pallas_call(
        _ada_kernel,
        out_shape=jax.ShapeDtypeStruct((rows, w_ada.shape[1]), F32),
        compiler_params=pltpu.CompilerParams(vmem_limit_bytes=VMEM_LIMIT),
        name="ada",
    )(c_all, w_ada, b_ada)


def _split_mod(mod):
    d = mod.shape[-1] // 3
    return mod[:, 0:d], mod[:, d:2 * d], mod[:, 2 * d:3 * d]


def _modulated_norm(x, mod, g_pre):
    shift, scale, _ = _split_mod(mod)
    return (_rms(x, g_pre) * (1.0 + scale) + shift).astype(BF16)


def _conv_branch(u, u1, u2, bc, gc, conv_w, g_conv):
    conv_y = conv_w[0:1] * u2 + conv_w[1:2] * u1 + conv_w[2:3] * u
    return _rms(bc * conv_y, g_conv) * (gc * jax.nn.sigmoid(gc))


def _prompt_proj_kernel(x_ref, mod_ref, gpre_ref, w_ref, cw_ref, gconv_ref,
                        ya_ref, q_ref, k_ref, kb_ref, v_ref, vb_ref, ga_ref,
                        conv_ref, u_scr):
    rows = x_ref.shape[1]
    cdim = cw_ref.shape[1]
    h = _modulated_norm(x_ref[0], mod_ref[0], gpre_ref[...])

    def seg(i):
        return jnp.dot(h, w_ref[:, i * cdim:(i + 1) * cdim],
                       preferred_element_type=F32)

    @pl.when(pl.program_id(1) == 0)
    def _():
        u_scr[0:8, :] = jnp.zeros((8, cdim), F32)

    u = seg(2) * seg(0)
    u_scr[8:8 + rows, :] = u
    u1 = u_scr[7:7 + rows, :]
    u2 = u_scr[6:6 + rows, :]
    ya = _conv_branch(u, u1, u2, seg(1), seg(3), cw_ref[...], gconv_ref[...])
    ya_ref[0] = ya.astype(BF16)
    tail = u_scr[rows + 6:rows + 8, :]
    conv_ref[0] = tail
    u_scr[6:8, :] = tail

    q = (seg(4) * (HEAD_DIM ** -0.5)).astype(BF16)
    k = seg(5)
    v = seg(6)
    k_ref[0] = k
    v_ref[0] = v
    kb = k.astype(BF16)
    vb = v.astype(BF16)
    for g in range(N_GROUPS):
        sl = slice(g * LANES, (g + 1) * LANES)
        q_ref[0, g] = q[:, sl]
        kb_ref[0, g] = kb[:, sl]
        vb_ref[0, g] = vb[:, sl]
    ga = seg(7)
    ga_ref[0] = ga * jax.nn.sigmoid(ga)


def _prompt_proj(x, mod, g_pre, w_in_b, conv_w, g_conv):
    B, S, D = x.shape
    cdim = conv_w.shape[1]
    rows = PROJ_ROWS
    grid = (B, S // rows)
    row_blk = lambda w: pl.BlockSpec((1, rows, w), lambda b, s: (b, s, 0))
    grp_blk = pl.BlockSpec((1, N_GROUPS, rows, LANES), lambda b, s: (b, 0, s, 0))
    const = lambda shp: pl.BlockSpec(shp, lambda b, s: (0,) * len(shp))
    grp_shape = jax.ShapeDtypeStruct((B, N_GROUPS, S, LANES), BF16)
    return pl.pallas_call(
        _prompt_proj_kernel,
        grid=grid,
        in_specs=[row_blk(D),
                  pl.BlockSpec((1, 1, 3 * D), lambda b, s: (b, 0, 0)),
                  const((1, D)), const(w_in_b.shape), const(conv_w.shape),
                  const((1, cdim))],
        out_specs=[row_blk(cdim), grp_blk, row_blk(ATT_DIM), grp_blk,
                   row_blk(ATT_DIM), grp_blk, row_blk(ATT_DIM),
                   pl.BlockSpec((1, CONV_WIDTH - 1, cdim), lambda b, s: (b, 0, 0))],
        out_shape=[jax.ShapeDtypeStruct((B, S, cdim), BF16), grp_shape,
                   jax.ShapeDtypeStruct((B, S, ATT_DIM), F32), grp_shape,
                   jax.ShapeDtypeStruct((B, S, ATT_DIM), F32), grp_shape,
                   jax.ShapeDtypeStruct((B, S, ATT_DIM), F32),
                   jax.ShapeDtypeStruct((B, CONV_WIDTH - 1, cdim), F32)],
        scratch_shapes=[pltpu.VMEM((rows + 8, cdim), F32)],
        compiler_params=pltpu.CompilerParams(
            dimension_semantics=("arbitrary", "arbitrary"),
            vmem_limit_bytes=VMEM_LIMIT),
        name="prompt_proj",
    )(x, mod, g_pre, w_in_b, conv_w, g_conv)


def _mixer_out(o, sga, ya, x, gate, g_attn, w_out_ref, g_post):
    cdim = ya.shape[-1]
    yb = (_rms(o, g_attn) * sga).astype(BF16)
    m = (jnp.dot(ya, w_out_ref[0:cdim, :], preferred_element_type=F32)
         + jnp.dot(yb, w_out_ref[cdim:, :], preferred_element_type=F32))
    return x + gate * _rms(m, g_post)


def _prompt_attn_kernel(b_ref, q_ref, k_ref, v_ref, tri_ref, ya_ref, sga_ref,
                        x_ref, mod_ref, gattn_ref, wout_ref, gpost_ref,
                        y_ref, acc_scr, carry_scr, o_scr):
    blk = q_ref.shape[2]
    i = pl.program_id(1)
    lane = lax.broadcasted_iota(jnp.int32, (blk, LANES), 1)
    row = lax.broadcasted_iota(jnp.int32, (blk, blk), 0)
    col = lax.broadcasted_iota(jnp.int32, (blk, blk), 1)
    causal = col < row
    tri2 = tri_ref[...]

    def scores(qh, kblk, bias):
        s = lax.dot_general(qh, kblk, (((1,), (1,)), ((), ())),
                            preferred_element_type=F32)
        return s + bias

    def suffix_sum(sp):
        hi, lo = _split_bf16(sp)
        return jnp.dot(jnp.concatenate([hi, lo], axis=1), tri2,
                       preferred_element_type=F32)

    def row_total(sp):
        t = jnp.sum(sp, axis=1, keepdims=True)
        return jnp.broadcast_to(t, (blk, LANES))

    def group_body(g, _):
        qg = q_ref[0, g]
        for hh in range(HEADS_PER_GROUP):
            in_head = (lane >= hh * HEAD_DIM) & (lane < (hh + 1) * HEAD_DIM)
            qh = jnp.where(in_head, qg, jnp.zeros_like(qg))
            bias = b_ref[g * HEADS_PER_GROUP + hh]

            kd = k_ref[0, g, pl.ds(pl.multiple_of(i * blk, blk), blk), :]
            vd = v_ref[0, g, pl.ds(pl.multiple_of(i * blk, blk), blk), :]
            z = scores(qh, kd, bias)
            sp = jnp.where(causal, _softplus(z), 0.0)
            a = jnp.where(causal, jnp.exp(z - suffix_sum(sp)), 0.0)
            acc_scr[hh] = jnp.dot(a.astype(BF16), vd, preferred_element_type=F32)
            carry_scr[hh] = row_total(sp)

            def key_body(jj, _):
                off = pl.multiple_of((i - 1 - jj) * blk, blk)
                kb = k_ref[0, g, pl.ds(off, blk), :]
                vb = v_ref[0, g, pl.ds(off, blk), :]
                z = scores(qh, kb, bias)
                sp = _softplus(z)
                c = carry_scr[hh]
                c2 = jnp.concatenate([c] * (blk // LANES), axis=1)
                a = jnp.exp(z - suffix_sum(sp) - c2)
                acc_scr[hh] += jnp.dot(a.astype(BF16), vb,
                                       preferred_element_type=F32)
                carry_scr[hh] = c + row_total(sp)
                return 0

            lax.fori_loop(0, i, key_body, 0)
        o_scr[g] = jnp.where(lane < HEAD_DIM, acc_scr[0], acc_scr[1])
        return 0

    lax.fori_loop(0, N_GROUPS, group_body, 0)

    o = jnp.concatenate([o_scr[g] for g in range(N_GROUPS)], axis=1)
    y_ref[0] = _mixer_out(o, sga_ref[0], ya_ref[0], x_ref[0],
                          _split_mod(mod_ref[0])[2], gattn_ref[...], wout_ref,
                          gpost_ref[...])


def _prompt_attn(b_sb, q4, k4, v4, tri2, ya, sga, x, mod, g_attn, w_out_b, g_post):
    B, S, D = x.shape
    blk = ATT_BLOCK
    cdim = ya.shape[-1]
    grid = (B, S // blk)
    row_blk = lambda w: pl.BlockSpec((1, blk, w), lambda b, i: (b, i, 0))
    const = lambda shp: pl.BlockSpec(shp, lambda b, i: (0,) * len(shp))
    whole_seq = pl.BlockSpec((1, N_GROUPS, S, LANES), lambda b, i: (b, 0, 0, 0))
    return pl.pallas_call(
        _prompt_attn_kernel,
        grid=grid,
        in_specs=[pl.BlockSpec(memory_space=pltpu.SMEM),
                  pl.BlockSpec((1, N_GROUPS, blk, LANES), lambda b, i: (b, 0, i, 0)),
                  whole_seq, whole_seq, const(tri2.shape),
                  row_blk(cdim), row_blk(ATT_DIM), row_blk(D),
                  pl.BlockSpec((1, 1, 3 * D), lambda b, i: (b, 0, 0)),
                  const((1, ATT_DIM)), const(w_out_b.shape), const((1, D))],
        out_specs=row_blk(D),
        out_shape=jax.ShapeDtypeStruct((B, S, D), F32),
        scratch_shapes=[pltpu.VMEM((HEADS_PER_GROUP, blk, LANES), F32),
                        pltpu.VMEM((HEADS_PER_GROUP, blk, LANES), F32),
                        pltpu.VMEM((N_GROUPS, blk, LANES), F32)],
        compiler_params=pltpu.CompilerParams(
            dimension_semantics=("arbitrary", "arbitrary"),
            vmem_limit_bytes=VMEM_LIMIT),
        name="prompt_attn",
    )(b_sb, q4, k4, v4, tri2, ya, sga, x, mod, g_attn, w_out_b, g_post)


def _sample_proj_kernel(x_ref, mod_ref, st_ref, gpre_ref, w_ref, cw_ref,
                        gconv_ref, ya_ref, q_ref, k_ref, v_ref, ga_ref, conv_ref):
    cdim = cw_ref.shape[1]
    h = _modulated_norm(x_ref[...], mod_ref[...], gpre_ref[...])
    p = jnp.dot(h, w_ref[...], preferred_element_type=F32)
    seg = lambda i: p[:, i * cdim:(i + 1) * cdim]
    u = seg(2) * seg(0)
    u2, u1 = st_ref[:, 0:cdim], st_ref[:, cdim:2 * cdim]
    ya_ref[...] = _conv_branch(u, u1, u2, seg(1), seg(3), cw_ref[...],
                               gconv_ref[...]).astype(BF16)
    conv_ref[:, 0:cdim] = u1
    conv_ref[:, cdim:2 * cdim] = u
    q_ref[...] = seg(4) * (HEAD_DIM ** -0.5)
    k_ref[...] = seg(5)
    v_ref[...] = seg(6)
    ga = seg(7)
    ga_ref[...] = ga * jax.nn.sigmoid(ga)


def _sample_proj(x, mod, state, g_pre, w_in_b, conv_w, g_conv):
    n, _ = x.shape
    cdim = conv_w.shape[1]
    sds = lambda w, dt=F32: jax.ShapeDtypeStruct((n, w), dt)
    return pl.pallas_call(
        _sample_proj_kernel,
        out_shape=[sds(cdim, BF16), sds(ATT_DIM), sds(ATT_DIM), sds(ATT_DIM),
                   sds(ATT_DIM), sds((CONV_WIDTH - 1) * cdim)],
        compiler_params=pltpu.CompilerParams(vmem_limit_bytes=VMEM_LIMIT),
        name="sample_proj",
    )(x, mod, state, g_pre, w_in_b, conv_w, g_conv)


def _sample_attn_kernel(pt_ref, qcol_ref, bias_ref, tri_ref,
                        ck_hbm, cv_hbm, o_ref,
                        kbuf, vbuf, sems, acc_scr, carry_scr, *, groups):
    g = pl.program_id(0)
    n_steps = pl.num_programs(0)
    page_tokens = kbuf.shape[3]
    seg_pages = KEY_CHUNK // page_tokens
    n_seg = PAGES_PER_STEP // seg_pages

    def copies(step, slot):
        seq = step // groups
        first = (groups - 1 - step % groups) * PAGES_PER_STEP
        out = []
        for p in range(PAGES_PER_STEP):
            page = pt_ref[seq, first + p]
            out.append(pltpu.make_async_copy(ck_hbm.at[page], kbuf.at[slot, p],
                                             sems.at[0, slot]))
            out.append(pltpu.make_async_copy(cv_hbm.at[page], vbuf.at[slot, p],
                                             sems.at[1, slot]))
        return out

    @pl.when(g == 0)
    def _():
        for c in copies(0, 0):
            c.start()

    @pl.when(g + 1 < n_steps)
    def _():
        for c in copies(g + 1, (g + 1) % 2):
            c.start()

    slot = g % 2
    for c in copies(g, slot):
        c.wait()

    @pl.when(g % groups == 0)
    def _():
        acc_scr[...] = jnp.zeros_like(acc_scr)
        carry_scr[...] = jnp.zeros_like(carry_scr)

    qcol = qcol_ref[0]
    bias = bias_ref[...]

    z_seg, sp_seg = [], []
    for s in range(n_seg):
        zp = []
        for p in range(s * seg_pages, (s + 1) * seg_pages):
            prod = kbuf[slot, p] * qcol
            zp.append(jnp.sum(prod.reshape(N_HEADS, HEAD_DIM, page_tokens), axis=1))
        z = jnp.concatenate(zp, axis=1) + jnp.concatenate([bias] * seg_pages, axis=1)
        z_seg.append(z)
        sp_seg.append(_softplus(z))

    hi, lo = _split_bf16(jnp.concatenate(sp_seg, axis=0))
    suffix = jnp.dot(jnp.concatenate([hi, lo], axis=1), tri_ref[...],
                     preferred_element_type=F32)

    carry = carry_scr[...]
    a_seg = [None] * n_seg
    for s in reversed(range(n_seg)):
        c = jnp.concatenate([carry] * seg_pages, axis=1)
        a_seg[s] = jnp.exp(z_seg[s] - suffix[s * N_HEADS:(s + 1) * N_HEADS] - c)
        total = jnp.sum(sp_seg[s], axis=1, keepdims=True)
        carry = carry + jnp.broadcast_to(total, carry.shape)
    carry_scr[...] = carry

    for h in range(N_HEADS):
        rows = slice(h * HEAD_DIM, (h + 1) * HEAD_DIM)
        acc = acc_scr[rows, :]
        for p in range(PAGES_PER_STEP):
            lanes = slice((p % seg_pages) * page_tokens, (p % seg_pages + 1) * page_tokens)
            a_row = a_seg[p // seg_pages][h:h + 1, lanes]
            acc = acc + vbuf[slot, p, rows, :] * a_row
        acc_scr[rows, :] = acc

    @pl.when(g % groups == groups - 1)
    def _():
        o_ref[0] = jnp.sum(acc_scr[...].T, axis=0, keepdims=True)


def _sample_attn(page_table, qcol, bias_col, tri2, cache_k, cache_v):
    n_seq, n_pages = page_table.shape
    page_tokens = cache_k.shape[2]
    groups = n_pages // PAGES_PER_STEP
    page_buf = pltpu.VMEM((2, PAGES_PER_STEP, ATT_DIM, page_tokens), F32)
    grid_spec = pltpu.PrefetchScalarGridSpec(
        num_scalar_prefetch=1,
        grid=(n_seq * groups,),
        in_specs=[pl.BlockSpec((1, ATT_DIM, LANES), lambda s, pt: (s // groups, 0, 0)),
                  pl.BlockSpec((N_HEADS, LANES), lambda s, pt: (0, 0)),
                  pl.BlockSpec(tri2.shape, lambda s, pt: (0, 0)),
                  pl.BlockSpec(memory_space=pl.ANY),
                  pl.BlockSpec(memory_space=pl.ANY)],
        out_specs=pl.BlockSpec((1, 1, ATT_DIM), lambda s, pt: (s // groups, 0, 0)),
        scratch_shapes=[page_buf, page_buf,
                        pltpu.SemaphoreType.DMA((2, 2)),
                        pltpu.VMEM((ATT_DIM, LANES), F32),
                        pltpu.VMEM((N_HEADS, LANES), F32)],
    )
    return pl.pallas_call(
        functools.partial(_sample_attn_kernel, groups=groups),
        grid_spec=grid_spec,
        out_shape=jax.ShapeDtypeStruct((n_seq, 1, ATT_DIM), F32),
        compiler_params=pltpu.CompilerParams(
            dimension_semantics=("arbitrary",),
            vmem_limit_bytes=VMEM_LIMIT),
        name="sample_attn",
    )(page_table, qcol, bias_col, tri2, cache_k, cache_v)


def _sample_out_kernel(o_ref, sga_ref, ya_ref, x_ref, mod_ref, gattn_ref,
                       wout_ref, gpost_ref, y_ref):
    y_ref[...] = _mixer_out(o_ref[...], sga_ref[...], ya_ref[...], x_ref[...],
                            _split_mod(mod_ref[...])[2], gattn_ref[...], wout_ref,
                            gpost_ref[...])


def _sample_out(o, sga, ya, x, mod, g_attn, w_out_b, g_post):
    return pl.pallas_call(
        _sample_out_kernel,
        out_shape=jax.ShapeDtypeStruct(x.shape, F32),
        compiler_params=pltpu.CompilerParams(vmem_limit_bytes=VMEM_LIMIT),
        name="sample_out",
    )(o, sga, ya, x, mod, g_attn, w_out_b, g_post)


def _layer(xp, xs, c_all, pool_k, pool_v, state, page_table, w_ada, b_ada, g_pre,
           w_in, conv_w, g_conv, b_sb, g_attn, w_out, g_post):
    B, S, D = xp.shape
    n_seq = xs.shape[0]
    row = lambda v: v.reshape(1, -1)
    w_in_b = w_in.astype(BF16)
    w_out_b = w_out.astype(BF16)

    mod = _ada(c_all, w_ada, row(b_ada))
    mod_p, mod_s = mod[:B].reshape(B, 1, 3 * D), mod[B:B + n_seq]

    ya, q4, k, kb4, v, vb4, sga, conv_p = _prompt_proj(
        xp, mod_p, row(g_pre), w_in_b, conv_w, row(g_conv))
    idx = jnp.arange(ATT_BLOCK)
    tri = (idx[:, None] >= idx[None, :]).astype(BF16)
    tri2 = jnp.concatenate([tri, tri], axis=0)
    yp = _prompt_attn(b_sb, q4, kb4, vb4, tri2,
                      ya, sga, xp, mod_p, row(g_attn), w_out_b, row(g_post))

    xs2 = xs.reshape(n_seq, D)
    ya_s, q_s, k_s, v_s, sga_s, conv_s = _sample_proj(
        xs2, mod_s, state.reshape(n_seq, -1), row(g_pre), w_in_b, conv_w,
        row(g_conv))
    conv_s = conv_s.reshape(state.shape)
    n_pool, page_tokens = pool_k.shape[:2]
    assert page_tokens == LANES and KEY_CHUNK == ATT_BLOCK
    pages = lambda pool: jnp.transpose(pool, (0, 2, 3, 1)).reshape(
        n_pool, ATT_DIM, page_tokens)
    qcol = jnp.broadcast_to(q_s[:, :, None], (n_seq, ATT_DIM, LANES))
    bias_col = jnp.broadcast_to(b_sb[:, None], (N_HEADS, LANES))
    o_s = _sample_attn(page_table, qcol, bias_col, tri2, pages(pool_k), pages(pool_v))
    ys = _sample_out(o_s.reshape(n_seq, ATT_DIM), sga_s, ya_s, xs2, mod_s,
                     row(g_attn), w_out_b, row(g_post))

    heads = lambda t, n, L: t.reshape(n, L, N_HEADS, HEAD_DIM)
    return (yp, ys.reshape(xs.shape), heads(k, B, S), heads(v, B, S), conv_p,
            heads(k_s, n_seq, 1), heads(v_s, n_seq, 1), conv_s)


def kernel(x_prompt, x_sample, c_prompt, c_sample, cache_k, cache_v, state_conv,
           page_table, w_ada, b_ada, g_pre, w_in, conv_w, g_conv, b_sb, g_attn,
           w_out, g_post):
    assert x_sample.shape[1] == 1, "decode step handles one new token per sequence"
    depth = w_in.shape[0]
    n_rows = c_prompt.shape[0] + c_sample.shape[0]
    pad = (-n_rows) % 8
    c_all = jnp.concatenate(
        [c_prompt, c_sample, jnp.zeros((pad, c_prompt.shape[1]), F32)], axis=0)
    xp, xs = x_prompt, x_sample
    outs = [[] for _ in range(6)]
    for l in range(depth):
        xp, xs, kp, vp, cp, ksn, vsn, csn = _layer(
            xp, xs, c_all, cache_k[l], cache_v[l], state_conv[l], page_table,
            w_ada[l], b_ada[l], g_pre[l], w_in[l], conv_w[l], g_conv[l], b_sb[l],
            g_attn[l], w_out[l], g_post[l])
        for lst, t in zip(outs, (kp, vp, cp, ksn, vsn, csn)):
            lst.append(t)
    return (xp, xs) + tuple(jnp.stack(t) for t in outs)
```

```python
import functools

import jax
import jax.numpy as jnp
from jax import lax
from jax.experimental import pallas as pl
from jax.experimental.pallas import tpu as pltpu

N_HEADS = 8
HEAD_DIM = 64
ATT_DIM = N_HEADS * HEAD_DIM
CONV_WIDTH = 3
EPS = 1e-6

LANES = 128
HEADS_PER_GROUP = LANES // HEAD_DIM
N_GROUPS = ATT_DIM // LANES

PROJ_ROWS = 512
ATT_BLOCK = 256
ATT_QUERY_TILE = 2 * ATT_BLOCK
PAGES_PER_STEP = 8
KEY_CHUNK = 256
VMEM_LIMIT = 52 * 1024 * 1024

F32 = jnp.float32
BF16 = jnp.bfloat16


def _rms(x, g):
    ms = jnp.mean(x * x, axis=-1, keepdims=True)
    return x * lax.rsqrt(ms + EPS) * g


def _softplus(z):
    return jnp.maximum(z, 0.0) + jnp.log(1.0 + jnp.exp(-jnp.abs(z)))


LOG2E = 1.4426950408889634


SOFTPLUS2_LINEAR_ABOVE = 64.0


def _softplus2(z2):
    return jnp.where(z2 > SOFTPLUS2_LINEAR_ABOVE, z2,
                     jnp.log2(1.0 + jnp.exp2(z2)))


def _split_bf16(x):
    hi = x.astype(BF16)
    lo = (x - hi.astype(F32)).astype(BF16)
    return hi, lo


def _ada_kernel(c_ref, w_ref, b_ref, o_ref):
    c = c_ref[...]
    a = c * jax.nn.sigmoid(c)
    o_ref[...] = jnp.dot(a, w_ref[...], preferred_element_type=F32,
                         precision=lax.Precision.HIGHEST) + b_ref[...]


def _ada(c_all, w_ada, b_ada):
    rows = c_all.shape[0]
    return pl.pallas_call(
        _ada_kernel,
        out_shape=jax.ShapeDtypeStruct((rows, w_ada.shape[1]), F32),
        compiler_params=pltpu.CompilerParams(vmem_limit_bytes=VMEM_LIMIT),
        name="ada",
    )(c_all, w_ada, b_ada)


def _split_mod(mod):
    d = mod.shape[-1] // 3
    return mod[:, 0:d], mod[:, d:2 * d], mod[:, 2 * d:3 * d]


def _modulated_norm(x, mod, g_pre):
    shift, scale, _ = _split_mod(mod)
    return (_rms(x, g_pre) * (1.0 + scale) + shift).astype(BF16)


def _conv_branch(u, u1, u2, bc, gc, conv_w, g_conv):
    conv_y = conv_w[0:1] * u2 + conv_w[1:2] * u1 + conv_w[2:3] * u
    return _rms(bc * conv_y, g_conv) * (gc * jax.nn.sigmoid(gc))


def _prompt_proj_kernel(x_ref, mod_ref, gpre_ref, w_ref, cw_ref, gconv_ref,
                        ya_ref, q_ref, k_ref, kb_ref, v_ref, vb_ref, ga_ref,
                        conv_ref, u_scr):
    rows = x_ref.shape[1]
    cdim = cw_ref.shape[1]
    h = _modulated_norm(x_ref[0], mod_ref[0], gpre_ref[...])

    def seg(i):
        return jnp.dot(h, w_ref[:, i * cdim:(i + 1) * cdim],
                       preferred_element_type=F32)

    @pl.when(pl.program_id(1) == 0)
    def _():
        u_scr[0:8, :] = jnp.zeros((8, cdim), F32)

    u = seg(2) * seg(0)
    u_scr[8:8 + rows, :] = u
    u1 = u_scr[7:7 + rows, :]
    u2 = u_scr[6:6 + rows, :]
    ya = _conv_branch(u, u1, u2, seg(1), seg(3), cw_ref[...], gconv_ref[...])
    ya_ref[0] = ya.astype(BF16)
    tail = u_scr[rows + 6:rows + 8, :]
    conv_ref[0] = tail
    u_scr[6:8, :] = tail

    q = (seg(4) * (HEAD_DIM ** -0.5)).astype(BF16)
    k = seg(5)
    v = seg(6)
    k_ref[0] = k
    v_ref[0] = v
    kb = k.astype(BF16)
    vb = v.astype(BF16)
    for g in range(N_GROUPS):
        sl = slice(g * LANES, (g + 1) * LANES)
        q_ref[0, g] = q[:, sl]
        kb_ref[0, g] = kb[:, sl]
        vb_ref[0, g] = vb[:, sl]
    ga = seg(7)
    ga_ref[0] = ga * jax.nn.sigmoid(ga)


def _prompt_proj(x, mod, g_pre, w_in_b, conv_w, g_conv):
    B, S, D = x.shape
    cdim = conv_w.shape[1]
    rows = PROJ_ROWS
    grid = (B, S // rows)
    row_blk = lambda w: pl.BlockSpec((1, rows, w), lambda b, s: (b, s, 0))
    grp_blk = pl.BlockSpec((1, N_GROUPS, rows, LANES), lambda b, s: (b, 0, s, 0))
    const = lambda shp: pl.BlockSpec(shp, lambda b, s: (0,) * len(shp))
    grp_shape = jax.ShapeDtypeStruct((B, N_GROUPS, S, LANES), BF16)
    return pl.pallas_call(
        _prompt_proj_kernel,
        grid=grid,
        in_specs=[row_blk(D),
                  pl.BlockSpec((1, 1, 3 * D), lambda b, s: (b, 0, 0)),
                  const((1, D)), const(w_in_b.shape), const(conv_w.shape),
                  const((1, cdim))],
        out_specs=[row_blk(cdim), grp_blk, row_blk(ATT_DIM), grp_blk,
                   row_blk(ATT_DIM), grp_blk, row_blk(ATT_DIM),
                   pl.BlockSpec((1, CONV_WIDTH - 1, cdim), lambda b, s: (b, 0, 0))],
        out_shape=[jax.ShapeDtypeStruct((B, S, cdim), BF16), grp_shape,
                   jax.ShapeDtypeStruct((B, S, ATT_DIM), F32), grp_shape,
                   jax.ShapeDtypeStruct((B, S, ATT_DIM), F32), grp_shape,
                   jax.ShapeDtypeStruct((B, S, ATT_DIM), F32),
                   jax.ShapeDtypeStruct((B, CONV_WIDTH - 1, cdim), F32)],
        scratch_shapes=[pltpu.VMEM((rows + 8, cdim), F32)],
        compiler_params=pltpu.CompilerParams(
            dimension_semantics=("arbitrary", "arbitrary"),
            vmem_limit_bytes=VMEM_LIMIT),
        name="prompt_proj",
    )(x, mod, g_pre, w_in_b, conv_w, g_conv)


def _mixer_out(o, sga, ya, x, gate, g_attn, w_out_ref, g_post):
    cdim = ya.shape[-1]
    yb = (_rms(o, g_attn) * sga).astype(BF16)
    m = (jnp.dot(ya, w_out_ref[0:cdim, :], preferred_element_type=F32)
         + jnp.dot(yb, w_out_ref[cdim:, :], preferred_element_type=F32))
    return x + gate * _rms(m, g_post)


def _prompt_attn_kernel(b_ref, q_ref, k_ref, v_ref, tri_ref, ya_ref, sga_ref,
                        x_ref, mod_ref, gattn_ref, wout_ref, gpost_ref,
                        y_ref, qm_scr, z_scr, a_scr, acc_scr, carry_scr, o_scr):
    blk = ATT_BLOCK
    halves = q_ref.shape[2] // blk
    chains = range(HEADS_PER_GROUP * halves)
    i = pl.program_id(1)
    lane = lax.broadcasted_iota(jnp.int32, (blk, LANES), 1)
    row = lax.broadcasted_iota(jnp.int32, (blk, blk), 0)
    col = lax.broadcasted_iota(jnp.int32, (blk, blk), 1)
    causal = col < row
    tri2 = tri_ref[...]

    def raw_scores(g, key_blk, active):
        off = pl.multiple_of(key_blk * blk, blk)
        kb = k_ref[0, g, pl.ds(off, blk), :]
        qs = jnp.concatenate([qm_scr[c] for c in active], axis=0)
        return lax.dot_general(qs, kb, (((1,), (1,)), ((), ())),
                               preferred_element_type=F32)

    def weights(g, s, active, masked):
        z, sp = [], []
        for n, c in enumerate(active):
            bias2 = b_ref[g * HEADS_PER_GROUP + c // halves] * LOG2E
            zc = s[n * blk:(n + 1) * blk] * LOG2E + bias2
            spc = _softplus2(zc)
            if c in masked:
                spc = jnp.where(causal, spc, 0.0)
            z.append(zc)
            sp.append(spc)
        hi, lo = _split_bf16(jnp.concatenate(sp, axis=0))
        suffix = jnp.dot(jnp.concatenate([hi, lo], axis=1), tri2,
                         preferred_element_type=F32)
        a = []
        for n, c in enumerate(active):
            carry = carry_scr[c]
            c2 = jnp.concatenate([carry] * (blk // LANES), axis=1)
            ac = jnp.exp2(z[n] - suffix[n * blk:(n + 1) * blk] - c2)
            if c in masked:
                ac = jnp.where(causal, ac, 0.0)
            a.append(ac.astype(BF16))
            total = jnp.sum(sp[n], axis=1, keepdims=True)
            carry_scr[c] = carry + jnp.broadcast_to(total, (blk, LANES))
        return jnp.concatenate(a, axis=0)

    def accumulate(g, key_blk, a, active):
        off = pl.multiple_of(key_blk * blk, blk)
        vb = v_ref[0, g, pl.ds(off, blk), :]
        o = jnp.dot(a, vb, preferred_element_type=F32)
        for n, c in enumerate(active):
            acc_scr[c] += o[n * blk:(n + 1) * blk]

    def group_body(g, _):
        qg = q_ref[0, g]
        for c in chains:
            hh, half = divmod(c, halves)
            in_head = (lane >= hh * HEAD_DIM) & (lane < (hh + 1) * HEAD_DIM)
            qh = qg[half * blk:(half + 1) * blk]
            qm_scr[c] = jnp.where(in_head, qh, jnp.zeros_like(qh))
        acc_scr[...] = jnp.zeros_like(acc_scr)
        carry_scr[...] = jnp.zeros_like(carry_scr)

        top = halves * i
        everyone = list(chains)
        z_scr[0] = raw_scores(g, jnp.maximum(top - 1, 0), everyone)

        for d in reversed(range(1, halves)):
            active = [c for c in chains if c % halves >= d]
            masked = {c for c in active if c % halves == d}
            a = weights(g, raw_scores(g, top + d, active), active, masked)
            accumulate(g, top + d, a, active)
        a_scr[1] = weights(g, raw_scores(g, top, everyone), everyone,
                           {c for c in chains if c % halves == 0})

        assert halves == 2

        def key_body(m, _):
            nxt = top - 1 - halves * m
            for cur in range(halves):
                accumulate(g, nxt - cur + 1, a_scr[1 - cur], everyone)
                z_scr[1 - cur] = raw_scores(g, jnp.maximum(nxt - cur - 1, 0), everyone)
                a_scr[cur] = weights(g, z_scr[cur], everyone, set())
            return 0

        lax.fori_loop(0, i, key_body, 0)
        accumulate(g, 0, a_scr[1], everyone)
        heads = [jnp.concatenate([acc_scr[hh * halves + half] for half in range(halves)],
                                 axis=0) for hh in range(HEADS_PER_GROUP)]
        lane_q = lax.broadcasted_iota(jnp.int32, heads[0].shape, 1)
        o_scr[g] = jnp.where(lane_q < HEAD_DIM, heads[0], heads[1])
        return 0

    lax.fori_loop(0, N_GROUPS, group_body, 0)

    o = jnp.concatenate([o_scr[g] for g in range(N_GROUPS)], axis=1)
    y_ref[0] = _mixer_out(o, sga_ref[0], ya_ref[0], x_ref[0],
                          _split_mod(mod_ref[0])[2], gattn_ref[...], wout_ref,
                          gpost_ref[...])


def _prompt_attn(b_sb, q4, k4, v4, tri2, ya, sga, x, mod, g_attn, w_out_b, g_post):
    B, S, D = x.shape
    blk = ATT_QUERY_TILE
    n_chain = HEADS_PER_GROUP * (blk // ATT_BLOCK)
    cdim = ya.shape[-1]
    grid = (B, S // blk)
    row_blk = lambda w: pl.BlockSpec((1, blk, w), lambda b, i: (b, i, 0))
    const = lambda shp: pl.BlockSpec(shp, lambda b, i: (0,) * len(shp))
    whole_seq = pl.BlockSpec((1, N_GROUPS, S, LANES), lambda b, i: (b, 0, 0, 0))
    return pl.pallas_call(
        _prompt_attn_kernel,
        grid=grid,
        in_specs=[pl.BlockSpec(memory_space=pltpu.SMEM),
                  pl.BlockSpec((1, N_GROUPS, blk, LANES), lambda b, i: (b, 0, i, 0)),
                  whole_seq, whole_seq, const(tri2.shape),
                  row_blk(cdim), row_blk(ATT_DIM), row_blk(D),
                  pl.BlockSpec((1, 1, 3 * D), lambda b, i: (b, 0, 0)),
                  const((1, ATT_DIM)), const(w_out_b.shape), const((1, D))],
        out_specs=row_blk(D),
        out_shape=jax.ShapeDtypeStruct((B, S, D), F32),
        scratch_shapes=[pltpu.VMEM((n_chain, ATT_BLOCK, LANES), BF16),
                        pltpu.VMEM((2, n_chain * ATT_BLOCK, ATT_BLOCK), F32),
                        pltpu.VMEM((2, n_chain * ATT_BLOCK, ATT_BLOCK), BF16),
                        pltpu.VMEM((n_chain, ATT_BLOCK, LANES), F32),
                        pltpu.VMEM((n_chain, ATT_BLOCK, LANES), F32),
                        pltpu.VMEM((N_GROUPS, blk, LANES), F32)],
        compiler_params=pltpu.CompilerParams(
            dimension_semantics=("arbitrary", "arbitrary"),
            vmem_limit_bytes=VMEM_LIMIT),
        name="prompt_attn",
    )(b_sb, q4, k4, v4, tri2, ya, sga, x, mod, g_attn, w_out_b, g_post)


def _sample_proj_kernel(x_ref, mod_ref, st_ref, gpre_ref, w_ref, cw_ref,
                        gconv_ref, ya_ref, q_ref, k_ref, v_ref, ga_ref, conv_ref):
    cdim = cw_ref.shape[1]
    h = _modulated_norm(x_ref[...], mod_ref[...], gpre_ref[...])
    p = jnp.dot(h, w_ref[...], preferred_element_type=F32)
    seg = lambda i: p[:, i * cdim:(i + 1) * cdim]
    u = seg(2) * seg(0)
    u2, u1 = st_ref[:, 0:cdim], st_ref[:, cdim:2 * cdim]
    ya_ref[...] = _conv_branch(u, u1, u2, seg(1), seg(3), cw_ref[...],
                               gconv_ref[...]).astype(BF16)
    conv_ref[:, 0:cdim] = u1
    conv_ref[:, cdim:2 * cdim] = u
    q_ref[...] = seg(4) * (HEAD_DIM ** -0.5)
    k_ref[...] = seg(5)
    v_ref[...] = seg(6)
    ga = seg(7)
    ga_ref[...] = ga * jax.nn.sigmoid(ga)


def _sample_proj(x, mod, state, g_pre, w_in_b, conv_w, g_conv):
    n, _ = x.shape
    cdim = conv_w.shape[1]
    sds = lambda w, dt=F32: jax.ShapeDtypeStruct((n, w), dt)
    return pl.pallas_call(
        _sample_proj_kernel,
        out_shape=[sds(cdim, BF16), sds(ATT_DIM), sds(ATT_DIM), sds(ATT_DIM),
                   sds(ATT_DIM), sds((CONV_WIDTH - 1) * cdim)],
        compiler_params=pltpu.CompilerParams(vmem_limit_bytes=VMEM_LIMIT),
        name="sample_proj",
    )(x, mod, state, g_pre, w_in_b, conv_w, g_conv)


def _sample_attn_kernel(pt_ref, qcol_ref, bias_ref, tri_ref,
                        ck_hbm, cv_hbm, o_ref,
                        kbuf, vbuf, sems, acc_scr, carry_scr, *, groups):
    g = pl.program_id(0)
    n_steps = pl.num_programs(0)
    page_tokens = kbuf.shape[3]
    seg_pages = KEY_CHUNK // page_tokens
    n_seg = PAGES_PER_STEP // seg_pages

    def copies(step, slot):
        seq = step // groups
        first = (groups - 1 - step % groups) * PAGES_PER_STEP
        out = []
        for p in range(PAGES_PER_STEP):
            page = pt_ref[seq, first + p]
            out.append(pltpu.make_async_copy(ck_hbm.at[page], kbuf.at[slot, p],
                                             sems.at[0, slot]))
            out.append(pltpu.make_async_copy(cv_hbm.at[page], vbuf.at[slot, p],
                                             sems.at[1, slot]))
        return out

    @pl.when(g == 0)
    def _():
        for c in copies(0, 0):
            c.start()

    @pl.when(g + 1 < n_steps)
    def _():
        for c in copies(g + 1, (g + 1) % 2):
            c.start()

    slot = g % 2
    for c in copies(g, slot):
        c.wait()

    @pl.when(g % groups == 0)
    def _():
        acc_scr[...] = jnp.zeros_like(acc_scr)
        carry_scr[...] = jnp.zeros_like(carry_scr)

    qcol = qcol_ref[0]
    bias = bias_ref[...]

    z_seg, sp_seg = [], []
    for s in range(n_seg):
        zp = []
        for p in range(s * seg_pages, (s + 1) * seg_pages):
            prod = kbuf[slot, p] * qcol
            zp.append(jnp.sum(prod.reshape(N_HEADS, HEAD_DIM, page_tokens), axis=1))
        z = jnp.concatenate(zp, axis=1) + jnp.concatenate([bias] * seg_pages, axis=1)
        z_seg.append(z)
        sp_seg.append(_softplus(z))

    hi, lo = _split_bf16(jnp.concatenate(sp_seg, axis=0))
    suffix = jnp.dot(jnp.concatenate([hi, lo], axis=1), tri_ref[...],
                     preferred_element_type=F32)

    carry = carry_scr[...]
    a_seg = [None] * n_seg
    for s in reversed(range(n_seg)):
        c = jnp.concatenate([carry] * seg_pages, axis=1)
        a_seg[s] = jnp.exp(z_seg[s] - suffix[s * N_HEADS:(s + 1) * N_HEADS] - c)
        total = jnp.sum(sp_seg[s], axis=1, keepdims=True)
        carry = carry + jnp.broadcast_to(total, carry.shape)
    carry_scr[...] = carry

    for h in range(N_HEADS):
        rows = slice(h * HEAD_DIM, (h + 1) * HEAD_DIM)
        acc = acc_scr[rows, :]
        for p in range(PAGES_PER_STEP):
            lanes = slice((p % seg_pages) * page_tokens, (p % seg_pages + 1) * page_tokens)
            a_row = a_seg[p // seg_pages][h:h + 1, lanes]
            acc = acc + vbuf[slot, p, rows, :] * a_row
        acc_scr[rows, :] = acc

    @pl.when(g % groups == groups - 1)
    def _():
        o_ref[0] = jnp.sum(acc_scr[...].T, axis=0, keepdims=True)


def _sample_attn(page_table, qcol, bias_col, tri2, cache_k, cache_v):
    n_seq, n_pages = page_table.shape
    page_tokens = cache_k.shape[2]
    groups = n_pages // PAGES_PER_STEP
    page_buf = pltpu.VMEM((2, PAGES_PER_STEP, ATT_DIM, page_tokens), F32)
    grid_spec = pltpu.PrefetchScalarGridSpec(
        num_scalar_prefetch=1,
        grid=(n_seq * groups,),
        in_specs=[pl.BlockSpec((1, ATT_DIM, LANES), lambda s, pt: (s // groups, 0, 0)),
                  pl.BlockSpec((N_HEADS, LANES), lambda s, pt: (0, 0)),
                  pl.BlockSpec(tri2.shape, lambda s, pt: (0, 0)),
                  pl.BlockSpec(memory_space=pl.ANY),
                  pl.BlockSpec(memory_space=pl.ANY)],
        out_specs=pl.BlockSpec((1, 1, ATT_DIM), lambda s, pt: (s // groups, 0, 0)),
        scratch_shapes=[page_buf, page_buf,
                        pltpu.SemaphoreType.DMA((2, 2)),
                        pltpu.VMEM((ATT_DIM, LANES), F32),
                        pltpu.VMEM((N_HEADS, LANES), F32)],
    )
    return pl.pallas_call(
        functools.partial(_sample_attn_kernel, groups=groups),
        grid_spec=grid_spec,
        out_shape=jax.ShapeDtypeStruct((n_seq, 1, ATT_DIM), F32),
        compiler_params=pltpu.CompilerParams(
            dimension_semantics=("arbitrary",),
            vmem_limit_bytes=VMEM_LIMIT),
        name="sample_attn",
    )(page_table, qcol, bias_col, tri2, cache_k, cache_v)


def _sample_out_kernel(o_ref, sga_ref, ya_ref, x_ref, mod_ref, gattn_ref,
                       wout_ref, gpost_ref, y_ref):
    y_ref[...] = _mixer_out(o_ref[...], sga_ref[...], ya_ref[...], x_ref[...],
                            _split_mod(mod_ref[...])[2], gattn_ref[...], wout_ref,
                            gpost_ref[...])


def _sample_out(o, sga, ya, x, mod, g_attn, w_out_b, g_post):
    return pl.pallas_call(
        _sample_out_kernel,
        out_shape=jax.ShapeDtypeStruct(x.shape, F32),
        compiler_params=pltpu.CompilerParams(vmem_limit_bytes=VMEM_LIMIT),
        name="sample_out",
    )(o, sga, ya, x, mod, g_attn, w_out_b, g_post)


def _layer(xp, xs, c_all, pool_k, pool_v, state, page_table, w_ada, b_ada, g_pre,
           w_in, conv_w, g_conv, b_sb, g_attn, w_out, g_post):
    B, S, D = xp.shape
    n_seq = xs.shape[0]
    row = lambda v: v.reshape(1, -1)
    w_in_b = w_in.astype(BF16)
    w_out_b = w_out.astype(BF16)

    mod = _ada(c_all, w_ada, row(b_ada))
    mod_p, mod_s = mod[:B].reshape(B, 1, 3 * D), mod[B:B + n_seq]

    ya, q4, k, kb4, v, vb4, sga, conv_p = _prompt_proj(
        xp, mod_p, row(g_pre), w_in_b, conv_w, row(g_conv))
    idx = jnp.arange(ATT_BLOCK)
    tri = (idx[:, None] >= idx[None, :]).astype(BF16)
    tri2 = jnp.concatenate([tri, tri], axis=0)
    yp = _prompt_attn(b_sb, q4, kb4, vb4, tri2,
                      ya, sga, xp, mod_p, row(g_attn), w_out_b, row(g_post))

    xs2 = xs.reshape(n_seq, D)
    ya_s, q_s, k_s, v_s, sga_s, conv_s = _sample_proj(
        xs2, mod_s, state.reshape(n_seq, -1), row(g_pre), w_in_b, conv_w,
        row(g_conv))
    conv_s = conv_s.reshape(state.shape)
    n_pool, page_tokens = pool_k.shape[:2]
    assert page_tokens == LANES and KEY_CHUNK == ATT_BLOCK
    pages = lambda pool: jnp.transpose(pool, (0, 2, 3, 1)).reshape(
        n_pool, ATT_DIM, page_tokens)
    qcol = jnp.broadcast_to(q_s[:, :, None], (n_seq, ATT_DIM, LANES))
    bias_col = jnp.broadcast_to(b_sb[:, None], (N_HEADS, LANES))
    o_s = _sample_attn(page_table, qcol, bias_col, tri2, pages(pool_k), pages(pool_v))
    ys = _sample_out(o_s.reshape(n_seq, ATT_DIM), sga_s, ya_s, xs2, mod_s,
                     row(g_attn), w_out_b, row(g_post))

    heads = lambda t, n, L: t.reshape(n, L, N_HEADS, HEAD_DIM)
    return (yp, ys.reshape(xs.shape), heads(k, B, S), heads(v, B, S), conv_p,
            heads(k_s, n_seq, 1), heads(v_s, n_seq, 1), conv_s)


def kernel(x_prompt, x_sample, c_prompt, c_sample, cache_k, cache_v, state_conv,
           page_table, w_ada, b_ada, g_pre, w_in, conv_w, g_conv, b_sb, g_attn,
           w_out, g_post):
    assert x_sample.shape[1] == 1, "decode step handles one new token per sequence"
    depth = w_in.shape[0]
    n_rows = c_prompt.shape[0] + c_sample.shape[0]
    pad = (-n_rows) % 8
    c_all = jnp.concatenate(
        [c_prompt, c_sample, jnp.zeros((pad, c_prompt.shape[1]), F32)], axis=0)
    xp, xs = x_prompt, x_sample
    outs = [[] for _ in range(6)]
    for l in range(depth):
        xp, xs, kp, vp, cp, ksn, vsn, csn = _layer(
            xp, xs, c_all, cache_k[l], cache_v[l], state_conv[l], page_table,
            w_ada[l], b_ada[l], g_pre[l], w_in[l], conv_w[l], g_conv[l], b_sb[l],
            g_attn[l], w_out[l], g_post[l])
        for lst, t in zip(outs, (kp, vp, cp, ksn, vsn, csn)):
            lst.append(t)
    return (xp, xs) + tuple(jnp.stack(t) for t in outs)
```

```python
import jax
import jax.numpy as jnp
from jax import lax
from jax.experimental import pallas as pl
from jax.experimental.pallas import tpu as pltpu

N_HEADS = 8
HEAD_DIM = 64
ATT_DIM = N_HEADS * HEAD_DIM
CONV_WIDTH = 3
EPS = 1e-6

LANES = 128
HEADS_PER_GROUP = LANES // HEAD_DIM
N_GROUPS = ATT_DIM // LANES

PROJ_ROWS = 512
ATT_BLOCK = 256
ATT_QUERY_TILE = 2 * ATT_BLOCK
PAGES_PER_CHUNK = 8
KEY_SEGMENT = ATT_BLOCK
VMEM_LIMIT = 56 * 1024 * 1024

F32 = jnp.float32
BF16 = jnp.bfloat16
LOG2E = 1.4426950408889634
SOFTPLUS2_LINEAR_ABOVE = 64.0


def _rms(x, g):
    ms = jnp.mean(x * x, axis=-1, keepdims=True)
    return x * lax.rsqrt(ms + EPS) * g


def _softplus(z):
    return jnp.maximum(z, 0.0) + jnp.log(1.0 + jnp.exp(-jnp.abs(z)))


def _softplus2(z2):
    return jnp.where(z2 > SOFTPLUS2_LINEAR_ABOVE, z2,
                     jnp.log2(1.0 + jnp.exp2(z2)))


def _split_bf16(x):
    hi = x.astype(BF16)
    lo = (x - hi.astype(F32)).astype(BF16)
    return hi, lo


def _suffix_sum(sp, tri2):
    hi, lo = _split_bf16(sp)
    return jnp.dot(jnp.concatenate([hi, lo], axis=1), tri2,
                   preferred_element_type=F32)


def _ada_kernel(c_ref, w_ref, b_ref, o_ref):
    c = c_ref[...]
    a = c * jax.nn.sigmoid(c)
    o_ref[...] = jnp.dot(a, w_ref[...], preferred_element_type=F32,
                         precision=lax.Precision.HIGHEST) + b_ref[...]


def _ada(c_all, w_ada, b_ada):
    rows = c_all.shape[0]
    return pl.pallas_call(
        _ada_kernel,
        out_shape=jax.ShapeDtypeStruct((rows, w_ada.shape[1]), F32),
        compiler_params=pltpu.CompilerParams(vmem_limit_bytes=VMEM_LIMIT),
        name="ada",
    )(c_all, w_ada, b_ada)


def _split_mod(mod):
    d = mod.shape[-1] // 3
    return mod[:, 0:d], mod[:, d:2 * d], mod[:, 2 * d:3 * d]


def _modulated_norm(x, mod, g_pre):
    shift, scale, _ = _split_mod(mod)
    return (_rms(x, g_pre) * (1.0 + scale) + shift).astype(BF16)


def _conv_branch(u, u1, u2, bc, gc, conv_w, g_conv):
    conv_y = conv_w[0:1] * u2 + conv_w[1:2] * u1 + conv_w[2:3] * u
    return _rms(bc * conv_y, g_conv) * (gc * jax.nn.sigmoid(gc))


def _prompt_proj_kernel(x_ref, mod_ref, gpre_ref, w_ref, cw_ref, gconv_ref,
                        ya_ref, q_ref, k_ref, kb_ref, v_ref, vb_ref, ga_ref,
                        conv_ref, u_scr):
    rows = x_ref.shape[1]
    cdim = cw_ref.shape[1]
    h = _modulated_norm(x_ref[0], mod_ref[0], gpre_ref[...])

    def seg(i):
        return jnp.dot(h, w_ref[:, i * cdim:(i + 1) * cdim],
                       preferred_element_type=F32)

    @pl.when(pl.program_id(1) == 0)
    def _():
        u_scr[0:8, :] = jnp.zeros((8, cdim), F32)

    u = seg(2) * seg(0)
    u_scr[8:8 + rows, :] = u
    u1 = u_scr[7:7 + rows, :]
    u2 = u_scr[6:6 + rows, :]
    ya = _conv_branch(u, u1, u2, seg(1), seg(3), cw_ref[...], gconv_ref[...])
    ya_ref[0] = ya.astype(BF16)
    tail = u_scr[rows + 6:rows + 8, :]
    conv_ref[0] = tail
    u_scr[6:8, :] = tail

    q = (seg(4) * (HEAD_DIM ** -0.5)).astype(BF16)
    k = seg(5)
    v = seg(6)
    k_ref[0] = k
    v_ref[0] = v
    kb = k.astype(BF16)
    vb = v.astype(BF16)
    for g in range(N_GROUPS):
        sl = slice(g * LANES, (g + 1) * LANES)
        q_ref[0, g] = q[:, sl]
        kb_ref[0, g] = kb[:, sl]
        vb_ref[0, g] = vb[:, sl]
    ga = seg(7)
    ga_ref[0] = ga * jax.nn.sigmoid(ga)


def _prompt_proj(x, mod, g_pre, w_in_b, conv_w, g_conv):
    B, S, D = x.shape
    cdim = conv_w.shape[1]
    rows = PROJ_ROWS
    grid = (B, S // rows)
    row_blk = lambda w: pl.BlockSpec((1, rows, w), lambda b, s: (b, s, 0))
    grp_blk = pl.BlockSpec((1, N_GROUPS, rows, LANES), lambda b, s: (b, 0, s, 0))
    const = lambda shp: pl.BlockSpec(shp, lambda b, s: (0,) * len(shp))
    grp_shape = jax.ShapeDtypeStruct((B, N_GROUPS, S, LANES), BF16)
    return pl.pallas_call(
        _prompt_proj_kernel,
        grid=grid,
        in_specs=[row_blk(D),
                  pl.BlockSpec((1, 1, 3 * D), lambda b, s: (b, 0, 0)),
                  const((1, D)), const(w_in_b.shape), const(conv_w.shape),
                  const((1, cdim))],
        out_specs=[row_blk(cdim), grp_blk, row_blk(ATT_DIM), grp_blk,
                   row_blk(ATT_DIM), grp_blk, row_blk(ATT_DIM),
                   pl.BlockSpec((1, CONV_WIDTH - 1, cdim), lambda b, s: (b, 0, 0))],
        out_shape=[jax.ShapeDtypeStruct((B, S, cdim), BF16), grp_shape,
                   jax.ShapeDtypeStruct((B, S, ATT_DIM), F32), grp_shape,
                   jax.ShapeDtypeStruct((B, S, ATT_DIM), F32), grp_shape,
                   jax.ShapeDtypeStruct((B, S, ATT_DIM), F32),
                   jax.ShapeDtypeStruct((B, CONV_WIDTH - 1, cdim), F32)],
        scratch_shapes=[pltpu.VMEM((rows + 8, cdim), F32)],
        compiler_params=pltpu.CompilerParams(
            dimension_semantics=("arbitrary", "arbitrary"),
            vmem_limit_bytes=VMEM_LIMIT),
        name="prompt_proj",
    )(x, mod, g_pre, w_in_b, conv_w, g_conv)


def _mixer_out(o, sga, ya, x, gate, g_attn, w_out_ref, g_post):
    cdim = ya.shape[-1]
    yb = (_rms(o, g_attn) * sga).astype(BF16)
    m = (jnp.dot(ya, w_out_ref[0:cdim, :], preferred_element_type=F32)
         + jnp.dot(yb, w_out_ref[cdim:, :], preferred_element_type=F32))
    return x + gate * _rms(m, g_post)


def _decode_chunk(kpages, vpages, qcol, bias, tri2, acc_ref, carry_ref):
    n_pages, _, page_tokens = kpages.shape
    seg_pages = KEY_SEGMENT // page_tokens
    n_seg = n_pages // seg_pages

    z_seg, sp_seg = [], []
    for s in range(n_seg):
        zp = []
        for p in range(s * seg_pages, (s + 1) * seg_pages):
            prod = kpages[p] * qcol
            zp.append(jnp.sum(prod.reshape(N_HEADS, HEAD_DIM, page_tokens), axis=1))
        z = jnp.concatenate(zp, axis=1) + jnp.concatenate([bias] * seg_pages, axis=1)
        z_seg.append(z)
        sp_seg.append(_softplus(z))

    suffix = _suffix_sum(jnp.concatenate(sp_seg, axis=0), tri2)

    carry = carry_ref[...]
    a_seg = [None] * n_seg
    for s in reversed(range(n_seg)):
        c = jnp.concatenate([carry] * seg_pages, axis=1)
        a_seg[s] = jnp.exp(z_seg[s] - suffix[s * N_HEADS:(s + 1) * N_HEADS] - c)
        total = jnp.sum(sp_seg[s], axis=1, keepdims=True)
        carry = carry + jnp.broadcast_to(total, carry.shape)
    carry_ref[...] = carry

    for h in range(N_HEADS):
        rows = slice(h * HEAD_DIM, (h + 1) * HEAD_DIM)
        acc = acc_ref[rows, :]
        for p in range(n_pages):
            lanes = slice((p % seg_pages) * page_tokens, (p % seg_pages + 1) * page_tokens)
            a_row = a_seg[p // seg_pages][h:h + 1, lanes]
            acc = acc + vpages[p, rows, :] * a_row
        acc_ref[rows, :] = acc


def _attn_kernel(pt_ref, b_ref, q_ref, k_ref, v_ref, tri_ref, ya_ref, sga_ref,
                 x_ref, mod_ref, gattn_ref, wout_ref, gpost_ref,
                 qcol_hbm, bcol_ref, ck_hbm, cv_hbm,
                 y_ref, os_ref,
                 qm_scr, z_scr, a_scr, acc_scr, carry_scr, o_scr,
                 kbuf, vbuf, qbuf, sems, dacc_scr, dcarry_scr, chunk_ref):
    blk = ATT_BLOCK
    halves = q_ref.shape[2] // blk
    chains = range(HEADS_PER_GROUP * halves)
    i = pl.program_id(1)
    lane = lax.broadcasted_iota(jnp.int32, (blk, LANES), 1)
    row = lax.broadcasted_iota(jnp.int32, (blk, blk), 0)
    col = lax.broadcasted_iota(jnp.int32, (blk, blk), 1)
    causal = col < row
    tri2 = tri_ref[...]

    n_seq, n_pages = pt_ref.shape
    groups = n_pages // PAGES_PER_CHUNK
    n_chunks = n_seq * groups

    def chunk_copies(c):
        slot = c % 2
        seq = c // groups
        first = (groups - 1 - c % groups) * PAGES_PER_CHUNK
        out = []
        for p in range(PAGES_PER_CHUNK):
            page = pt_ref[seq, first + p]
            out.append(pltpu.make_async_copy(ck_hbm.at[page], kbuf.at[slot, p],
                                             sems.at[0, slot]))
            out.append(pltpu.make_async_copy(cv_hbm.at[page], vbuf.at[slot, p],
                                             sems.at[1, slot]))
        return out

    def query_copy(c):
        seq = c // groups
        return pltpu.make_async_copy(qcol_hbm.at[seq], qbuf.at[seq % 2],
                                     sems.at[2, seq % 2])

    def start_chunk(c):
        for cp in chunk_copies(c):
            cp.start()

        @pl.when(c % groups == 0)
        def _():
            query_copy(c).start()

    def wait_chunk(c):
        for cp in chunk_copies(c):
            cp.wait()

        @pl.when(c % groups == 0)
        def _():
            query_copy(c).wait()

    def decode_chunk(c):
        @pl.when(c + 1 < n_chunks)
        def _():
            start_chunk(c + 1)

        wait_chunk(c)
        slot = c % 2
        seq = c // groups

        @pl.when(c % groups == 0)
        def _():
            dacc_scr[...] = jnp.zeros_like(dacc_scr)
            dcarry_scr[...] = jnp.zeros_like(dcarry_scr)

        _decode_chunk(kbuf.at[slot], vbuf.at[slot], qbuf[seq % 2], bcol_ref[...],
                      tri2, dacc_scr, dcarry_scr)

        @pl.when(c % groups == groups - 1)
        def _():
            os_ref[pl.ds(seq, 1), :] = jnp.sum(dacc_scr[...].T, axis=0, keepdims=True)

    def decode_slot():
        c = chunk_ref[0]

        @pl.when(c < n_chunks)
        def _():
            decode_chunk(c)
            chunk_ref[0] = c + 1

    first_step = (pl.program_id(0) == 0) & (i == 0)
    last_step = ((pl.program_id(0) == pl.num_programs(0) - 1)
                 & (i == pl.num_programs(1) - 1))

    @pl.when(first_step)
    def _():
        chunk_ref[0] = 0
        start_chunk(0)

    def raw_scores(g, key_blk, active):
        off = pl.multiple_of(key_blk * blk, blk)
        kb = k_ref[0, g, pl.ds(off, blk), :]
        qs = jnp.concatenate([qm_scr[c] for c in active], axis=0)
        return lax.dot_general(qs, kb, (((1,), (1,)), ((), ())),
                               preferred_element_type=F32)

    def weights(g, s, active, masked):
        z, sp = [], []
        for n, c in enumerate(active):
            bias2 = b_ref[g * HEADS_PER_GROUP + c // halves] * LOG2E
            zc = s[n * blk:(n + 1) * blk] * LOG2E + bias2
            spc = _softplus2(zc)
            if c in masked:
                spc = jnp.where(causal, spc, 0.0)
            z.append(zc)
            sp.append(spc)
        suffix = _suffix_sum(jnp.concatenate(sp, axis=0), tri2)
        a = []
        for n, c in enumerate(active):
            carry = carry_scr[c]
            c2 = jnp.concatenate([carry] * (blk // LANES), axis=1)
            ac = jnp.exp2(z[n] - suffix[n * blk:(n + 1) * blk] - c2)
            if c in masked:
                ac = jnp.where(causal, ac, 0.0)
            a.append(ac.astype(BF16))
            total = jnp.sum(sp[n], axis=1, keepdims=True)
            carry_scr[c] = carry + jnp.broadcast_to(total, (blk, LANES))
        return jnp.concatenate(a, axis=0)

    def accumulate(g, key_blk, a, active):
        off = pl.multiple_of(key_blk * blk, blk)
        vb = v_ref[0, g, pl.ds(off, blk), :]
        o = jnp.dot(a, vb, preferred_element_type=F32)
        for n, c in enumerate(active):
            acc_scr[c] += o[n * blk:(n + 1) * blk]

    def group_body(g, _):
        qg = q_ref[0, g]
        for c in chains:
            hh, half = divmod(c, halves)
            in_head = (lane >= hh * HEAD_DIM) & (lane < (hh + 1) * HEAD_DIM)
            qh = qg[half * blk:(half + 1) * blk]
            qm_scr[c] = jnp.where(in_head, qh, jnp.zeros_like(qh))
        acc_scr[...] = jnp.zeros_like(acc_scr)
        carry_scr[...] = jnp.zeros_like(carry_scr)

        top = halves * i
        everyone = list(chains)
        z_scr[0] = raw_scores(g, jnp.maximum(top - 1, 0), everyone)

        for d in reversed(range(1, halves)):
            active = [c for c in chains if c % halves >= d]
            masked = {c for c in active if c % halves == d}
            a = weights(g, raw_scores(g, top + d, active), active, masked)
            accumulate(g, top + d, a, active)
        a_scr[1] = weights(g, raw_scores(g, top, everyone), everyone,
                           {c for c in chains if c % halves == 0})
        decode_slot()

        assert halves == 2

        def key_body(m, _):
            nxt = top - 1 - halves * m
            for cur in range(halves):
                accumulate(g, nxt - cur + 1, a_scr[1 - cur], everyone)
                z_scr[1 - cur] = raw_scores(g, jnp.maximum(nxt - cur - 1, 0), everyone)
                a_scr[cur] = weights(g, z_scr[cur], everyone, set())
            decode_slot()
            return 0

        lax.fori_loop(0, i, key_body, 0)
        accumulate(g, 0, a_scr[1], everyone)
        heads = [jnp.concatenate([acc_scr[hh * halves + half] for half in range(halves)],
                                 axis=0) for hh in range(HEADS_PER_GROUP)]
        lane_q = lax.broadcasted_iota(jnp.int32, heads[0].shape, 1)
        o_scr[g] = jnp.where(lane_q < HEAD_DIM, heads[0], heads[1])
        return 0

    lax.fori_loop(0, N_GROUPS, group_body, 0)

    o = jnp.concatenate([o_scr[g] for g in range(N_GROUPS)], axis=1)
    y_ref[0] = _mixer_out(o, sga_ref[0], ya_ref[0], x_ref[0],
                          _split_mod(mod_ref[0])[2], gattn_ref[...], wout_ref,
                          gpost_ref[...])

    @pl.when(last_step)
    def _():
        def drain(c, _):
            decode_chunk(c)
            return 0

        lax.fori_loop(chunk_ref[0], n_chunks, drain, 0)
        chunk_ref[0] = n_chunks


def _attention(page_table, b_sb, q4, k4, v4, tri2, ya, sga, x, mod, g_attn, w_out_b,
               g_post, qcol, bias_col, cache_k, cache_v):
    B, S, D = x.shape
    n_seq, n_pages = page_table.shape
    page_tokens = cache_k.shape[2]
    assert n_pages % PAGES_PER_CHUNK == 0 and KEY_SEGMENT % page_tokens == 0
    blk = ATT_QUERY_TILE
    n_chain = HEADS_PER_GROUP * (blk // ATT_BLOCK)
    cdim = ya.shape[-1]
    row_blk = lambda w: pl.BlockSpec((1, blk, w), lambda b, i, pt: (b, i, 0))
    once = pl.Buffered(1)
    const = lambda shp: pl.BlockSpec(shp, lambda b, i, pt: (0,) * len(shp),
                                     pipeline_mode=once)
    whole_seq = pl.BlockSpec((1, N_GROUPS, S, LANES), lambda b, i, pt: (b, 0, 0, 0),
                             pipeline_mode=once)
    page_buf = pltpu.VMEM((2, PAGES_PER_CHUNK, ATT_DIM, page_tokens), F32)
    grid_spec = pltpu.PrefetchScalarGridSpec(
        num_scalar_prefetch=1,
        grid=(B, S // blk),
        in_specs=[pl.BlockSpec(memory_space=pltpu.SMEM),
                  pl.BlockSpec((1, N_GROUPS, blk, LANES), lambda b, i, pt: (b, 0, i, 0)),
                  whole_seq, whole_seq, const(tri2.shape),
                  row_blk(cdim), row_blk(ATT_DIM), row_blk(D),
                  pl.BlockSpec((1, 1, 3 * D), lambda b, i, pt: (b, 0, 0)),
                  const((1, ATT_DIM)), const(w_out_b.shape), const((1, D)),
                  pl.BlockSpec(memory_space=pl.ANY),
                  const(bias_col.shape),
                  pl.BlockSpec(memory_space=pl.ANY),
                  pl.BlockSpec(memory_space=pl.ANY)],
        out_specs=[row_blk(D),
                   pl.BlockSpec((n_seq, ATT_DIM), lambda b, i, pt: (0, 0))],
        scratch_shapes=[pltpu.VMEM((n_chain, ATT_BLOCK, LANES), BF16),
                        pltpu.VMEM((2, n_chain * ATT_BLOCK, ATT_BLOCK), F32),
                        pltpu.VMEM((2, n_chain * ATT_BLOCK, ATT_BLOCK), BF16),
                        pltpu.VMEM((n_chain, ATT_BLOCK, LANES), F32),
                        pltpu.VMEM((n_chain, ATT_BLOCK, LANES), F32),
                        pltpu.VMEM((N_GROUPS, blk, LANES), F32),
                        page_buf, page_buf,
                        pltpu.VMEM((2, ATT_DIM, LANES), F32),
                        pltpu.SemaphoreType.DMA((3, 2)),
                        pltpu.VMEM((ATT_DIM, LANES), F32),
                        pltpu.VMEM((N_HEADS, LANES), F32),
                        pltpu.SMEM((1,), jnp.int32)],
    )
    return pl.pallas_call(
        _attn_kernel,
        grid_spec=grid_spec,
        out_shape=[jax.ShapeDtypeStruct((B, S, D), F32),
                   jax.ShapeDtypeStruct((n_seq, ATT_DIM), F32)],
        compiler_params=pltpu.CompilerParams(
            dimension_semantics=("arbitrary", "arbitrary"),
            vmem_limit_bytes=VMEM_LIMIT),
        name="attention",
    )(page_table, b_sb, q4, k4, v4, tri2, ya, sga, x, mod, g_attn, w_out_b, g_post,
      qcol, bias_col, cache_k, cache_v)


def _sample_proj_kernel(x_ref, mod_ref, st_ref, gpre_ref, w_ref, cw_ref,
                        gconv_ref, ya_ref, q_ref, k_ref, v_ref, ga_ref, conv_ref):
    cdim = cw_ref.shape[1]
    h = _modulated_norm(x_ref[...], mod_ref[...], gpre_ref[...])
    p = jnp.dot(h, w_ref[...], preferred_element_type=F32)
    seg = lambda i: p[:, i * cdim:(i + 1) * cdim]
    u = seg(2) * seg(0)
    u2, u1 = st_ref[:, 0:cdim], st_ref[:, cdim:2 * cdim]
    ya_ref[...] = _conv_branch(u, u1, u2, seg(1), seg(3), cw_ref[...],
                               gconv_ref[...]).astype(BF16)
    conv_ref[:, 0:cdim] = u1
    conv_ref[:, cdim:2 * cdim] = u
    q_ref[...] = seg(4) * (HEAD_DIM ** -0.5)
    k_ref[...] = seg(5)
    v_ref[...] = seg(6)
    ga = seg(7)
    ga_ref[...] = ga * jax.nn.sigmoid(ga)


def _sample_proj(x, mod, state, g_pre, w_in_b, conv_w, g_conv):
    n, _ = x.shape
    cdim = conv_w.shape[1]
    sds = lambda w, dt=F32: jax.ShapeDtypeStruct((n, w), dt)
    return pl.pallas_call(
        _sample_proj_kernel,
        out_shape=[sds(cdim, BF16), sds(ATT_DIM), sds(ATT_DIM), sds(ATT_DIM),
                   sds(ATT_DIM), sds((CONV_WIDTH - 1) * cdim)],
        compiler_params=pltpu.CompilerParams(vmem_limit_bytes=VMEM_LIMIT),
        name="sample_proj",
    )(x, mod, state, g_pre, w_in_b, conv_w, g_conv)


def _sample_out_kernel(o_ref, sga_ref, ya_ref, x_ref, mod_ref, gattn_ref,
                       wout_ref, gpost_ref, y_ref):
    y_ref[...] = _mixer_out(o_ref[...], sga_ref[...], ya_ref[...], x_ref[...],
                            _split_mod(mod_ref[...])[2], gattn_ref[...], wout_ref,
                            gpost_ref[...])


def _sample_out(o, sga, ya, x, mod, g_attn, w_out_b, g_post):
    return pl.pallas_call(
        _sample_out_kernel,
        out_shape=jax.ShapeDtypeStruct(x.shape, F32),
        compiler_params=pltpu.CompilerParams(vmem_limit_bytes=VMEM_LIMIT),
        name="sample_out",
    )(o, sga, ya, x, mod, g_attn, w_out_b, g_post)


def _layer(xp, xs, c_all, pool_k, pool_v, state, page_table, w_ada, b_ada, g_pre,
           w_in, conv_w, g_conv, b_sb, g_attn, w_out, g_post):
    B, S, D = xp.shape
    n_seq = xs.shape[0]
    row = lambda v: v.reshape(1, -1)
    w_in_b = w_in.astype(BF16)
    w_out_b = w_out.astype(BF16)

    mod = _ada(c_all, w_ada, row(b_ada))
    mod_p, mod_s = mod[:B].reshape(B, 1, 3 * D), mod[B:B + n_seq]

    ya, q4, k, kb4, v, vb4, sga, conv_p = _prompt_proj(
        xp, mod_p, row(g_pre), w_in_b, conv_w, row(g_conv))
    xs2 = xs.reshape(n_seq, D)
    ya_s, q_s, k_s, v_s, sga_s, conv_s = _sample_proj(
        xs2, mod_s, state.reshape(n_seq, -1), row(g_pre), w_in_b, conv_w,
        row(g_conv))
    conv_s = conv_s.reshape(state.shape)

    idx = jnp.arange(ATT_BLOCK)
    tri = (idx[:, None] >= idx[None, :]).astype(BF16)
    tri2 = jnp.concatenate([tri, tri], axis=0)
    n_pool, page_tokens = pool_k.shape[:2]
    assert page_tokens == LANES
    pages = lambda pool: jnp.transpose(pool, (0, 2, 3, 1)).reshape(
        n_pool, ATT_DIM, page_tokens)
    qcol = jnp.broadcast_to(q_s[:, :, None], (n_seq, ATT_DIM, LANES))
    bias_col = jnp.broadcast_to(b_sb[:, None], (N_HEADS, LANES))
    yp, o_s = _attention(page_table, b_sb, q4, kb4, vb4, tri2, ya, sga, xp, mod_p,
                         row(g_attn), w_out_b, row(g_post), qcol, bias_col,
                         pages(pool_k), pages(pool_v))
    ys = _sample_out(o_s, sga_s, ya_s, xs2, mod_s, row(g_attn), w_out_b, row(g_post))

    heads = lambda t, n, L: t.reshape(n, L, N_HEADS, HEAD_DIM)
    return (yp, ys.reshape(xs.shape), heads(k, B, S), heads(v, B, S), conv_p,
            heads(k_s, n_seq, 1), heads(v_s, n_seq, 1), conv_s)


def kernel(x_prompt, x_sample, c_prompt, c_sample, cache_k, cache_v, state_conv,
           page_table, w_ada, b_ada, g_pre, w_in, conv_w, g_conv, b_sb, g_attn,
           w_out, g_post):
    assert x_sample.shape[1] == 1, "decode step handles one new token per sequence"
    depth = w_in.shape[0]
    n_rows = c_prompt.shape[0] + c_sample.shape[0]
    pad = (-n_rows) % 8
    c_all = jnp.concatenate(
        [c_prompt, c_sample, jnp.zeros((pad, c_prompt.shape[1]), F32)], axis=0)
    xp, xs = x_prompt, x_sample
    outs = [[] for _ in range(6)]
    for l in range(depth):
        xp, xs, kp, vp, cp, ksn, vsn, csn = _layer(
            xp, xs, c_all, cache_k[l], cache_v[l], state_conv[l], page_table,
            w_ada[l], b_ada[l], g_pre[l], w_in[l], conv_w[l], g_conv[l], b_sb[l],
            g_attn[l], w_out[l], g_post[l])
        for lst, t in zip(outs, (kp, vp, cp, ksn, vsn, csn)):
            lst.append(t)
    return (xp, xs) + tuple(jnp.stack(t) for t in outs)
```

```python
import functools

import jax
import jax.numpy as jnp
from jax import lax
from jax.experimental import pallas as pl
from jax.experimental.pallas import tpu as pltpu

N_HEADS = 8
HEAD_DIM = 64
ATT_DIM = N_HEADS * HEAD_DIM
CONV_WIDTH = 3
EPS = 1e-6

LANES = 128
HEADS_PER_GROUP = LANES // HEAD_DIM
N_GROUPS = ATT_DIM // LANES

PROJ_ROWS = 512
ATT_BLOCK = 256
ATT_QUERY_TILE = 2 * ATT_BLOCK
PAGES_PER_CHUNK = 8
KEY_SEGMENT = ATT_BLOCK
VMEM_LIMIT = 56 * 1024 * 1024

F32 = jnp.float32
BF16 = jnp.bfloat16
LOG2E = 1.4426950408889634
SOFTPLUS2_LINEAR_ABOVE = 64.0


def _rms(x, g):
    ms = jnp.mean(x * x, axis=-1, keepdims=True)
    return x * lax.rsqrt(ms + EPS) * g


def _softplus(z):
    return jnp.maximum(z, 0.0) + jnp.log(1.0 + jnp.exp(-jnp.abs(z)))


def _softplus2(z2):
    return jnp.where(z2 > SOFTPLUS2_LINEAR_ABOVE, z2,
                     jnp.log2(1.0 + jnp.exp2(z2)))


def _split_bf16(x):
    hi = x.astype(BF16)
    lo = (x - hi.astype(F32)).astype(BF16)
    return hi, lo


def _suffix_sum(sp, tri2):
    hi, lo = _split_bf16(sp)
    return jnp.dot(jnp.concatenate([hi, lo], axis=1), tri2,
                   preferred_element_type=F32)


def _ada_kernel(c_ref, w_ref, b_ref, o_ref):
    c = c_ref[...]
    a = c * jax.nn.sigmoid(c)
    o_ref[...] = jnp.dot(a, w_ref[...], preferred_element_type=F32,
                         precision=lax.Precision.HIGHEST) + b_ref[...]


def _ada(c_all, w_ada, b_ada):
    rows = c_all.shape[0]
    return pl.pallas_call(
        _ada_kernel,
        out_shape=jax.ShapeDtypeStruct((rows, w_ada.shape[1]), F32),
        compiler_params=pltpu.CompilerParams(vmem_limit_bytes=VMEM_LIMIT),
        name="ada",
    )(c_all, w_ada, b_ada)


def _split_mod(mod):
    d = mod.shape[-1] // 3
    return mod[:, 0:d], mod[:, d:2 * d], mod[:, 2 * d:3 * d]


def _modulated_norm(x, mod, g_pre):
    shift, scale, _ = _split_mod(mod)
    return (_rms(x, g_pre) * (1.0 + scale) + shift).astype(BF16)


def _conv_branch(u, u1, u2, bc, gc, conv_w, g_conv):
    conv_y = conv_w[0:1] * u2 + conv_w[1:2] * u1 + conv_w[2:3] * u
    return _rms(bc * conv_y, g_conv) * (gc * jax.nn.sigmoid(gc))


def _prompt_proj_kernel(x_ref, mod_ref, gpre_ref, w_ref, cw_ref, gconv_ref,
                        ya_ref, q_ref, k_ref, kb_ref, v_ref, vb_ref, ga_ref,
                        conv_ref, u_scr):
    rows = x_ref.shape[1]
    cdim = cw_ref.shape[1]
    h = _modulated_norm(x_ref[0], mod_ref[0], gpre_ref[...])

    def seg(i):
        return jnp.dot(h, w_ref[:, i * cdim:(i + 1) * cdim],
                       preferred_element_type=F32)

    @pl.when(pl.program_id(1) == 0)
    def _():
        u_scr[0:8, :] = jnp.zeros((8, cdim), F32)

    u = seg(2) * seg(0)
    u_scr[8:8 + rows, :] = u
    u1 = u_scr[7:7 + rows, :]
    u2 = u_scr[6:6 + rows, :]
    ya = _conv_branch(u, u1, u2, seg(1), seg(3), cw_ref[...], gconv_ref[...])
    ya_ref[0] = ya.astype(BF16)
    tail = u_scr[rows + 6:rows + 8, :]
    conv_ref[0] = tail
    u_scr[6:8, :] = tail

    q = (seg(4) * (HEAD_DIM ** -0.5)).astype(BF16)
    k = seg(5)
    v = seg(6)
    k_ref[0] = k
    v_ref[0] = v
    kb = k.astype(BF16)
    vb = v.astype(BF16)
    for g in range(N_GROUPS):
        sl = slice(g * LANES, (g + 1) * LANES)
        q_ref[0, g] = q[:, sl]
        kb_ref[0, g] = kb[:, sl]
        vb_ref[0, g] = vb[:, sl]
    ga = seg(7)
    ga_ref[0] = ga * jax.nn.sigmoid(ga)


def _prompt_proj(x, mod, g_pre, w_in_b, conv_w, g_conv):
    B, S, D = x.shape
    cdim = conv_w.shape[1]
    rows = PROJ_ROWS
    grid = (B, S // rows)
    row_blk = lambda w: pl.BlockSpec((1, rows, w), lambda b, s: (b, s, 0))
    grp_blk = pl.BlockSpec((1, N_GROUPS, rows, LANES), lambda b, s: (b, 0, s, 0))
    const = lambda shp: pl.BlockSpec(shp, lambda b, s: (0,) * len(shp))
    grp_shape = jax.ShapeDtypeStruct((B, N_GROUPS, S, LANES), BF16)
    return pl.pallas_call(
        _prompt_proj_kernel,
        grid=grid,
        in_specs=[row_blk(D),
                  pl.BlockSpec((1, 1, 3 * D), lambda b, s: (b, 0, 0)),
                  const((1, D)), const(w_in_b.shape), const(conv_w.shape),
                  const((1, cdim))],
        out_specs=[row_blk(cdim), grp_blk, row_blk(ATT_DIM), grp_blk,
                   row_blk(ATT_DIM), grp_blk, row_blk(ATT_DIM),
                   pl.BlockSpec((1, CONV_WIDTH - 1, cdim), lambda b, s: (b, 0, 0))],
        out_shape=[jax.ShapeDtypeStruct((B, S, cdim), BF16), grp_shape,
                   jax.ShapeDtypeStruct((B, S, ATT_DIM), F32), grp_shape,
                   jax.ShapeDtypeStruct((B, S, ATT_DIM), F32), grp_shape,
                   jax.ShapeDtypeStruct((B, S, ATT_DIM), F32),
                   jax.ShapeDtypeStruct((B, CONV_WIDTH - 1, cdim), F32)],
        scratch_shapes=[pltpu.VMEM((rows + 8, cdim), F32)],
        compiler_params=pltpu.CompilerParams(
            dimension_semantics=("arbitrary", "arbitrary"),
            vmem_limit_bytes=VMEM_LIMIT),
        name="prompt_proj",
    )(x, mod, g_pre, w_in_b, conv_w, g_conv)


def _mixer_out(o, sga, ya, x, gate, g_attn, w_out_ref, g_post):
    cdim = ya.shape[-1]
    yb = (_rms(o, g_attn) * sga).astype(BF16)
    m = (jnp.dot(ya, w_out_ref[0:cdim, :], preferred_element_type=F32)
         + jnp.dot(yb, w_out_ref[cdim:, :], preferred_element_type=F32))
    return x + gate * _rms(m, g_post)


def _decode_keys(kpages, qcol_ref, bias, tri2, carry_ref, fresh):
    n_pages, _, page_tokens = kpages.shape
    seg_pages = KEY_SEGMENT // page_tokens
    n_seg = n_pages // seg_pages
    sub = 8

    z_rows = [[None] * N_HEADS for _ in range(n_pages)]
    for h in range(N_HEADS):
        part = [None] * n_pages
        for r in range(HEAD_DIM // sub):
            rows = slice(h * HEAD_DIM + r * sub, h * HEAD_DIM + (r + 1) * sub)
            q = qcol_ref[rows, :]
            for p in range(n_pages):
                t = kpages[p, rows, :] * q
                part[p] = t if r == 0 else part[p] + t
        for p in range(n_pages):
            z_rows[p][h] = jnp.sum(part[p], axis=0, keepdims=True)
    z_seg, sp_seg = [], []
    for s in range(n_seg):
        z = jnp.concatenate(
            [jnp.concatenate(z_rows[p], axis=0)
             for p in range(s * seg_pages, (s + 1) * seg_pages)], axis=1)
        z = z + jnp.concatenate([bias] * seg_pages, axis=1)
        z_seg.append(z)
        sp_seg.append(_softplus(z))

    suffix = _suffix_sum(jnp.concatenate(sp_seg, axis=0), tri2)

    carry = jnp.where(fresh, 0.0, carry_ref[...])
    a_seg = [None] * n_seg
    for s in reversed(range(n_seg)):
        c = jnp.concatenate([carry] * seg_pages, axis=1)
        a_seg[s] = jnp.exp(z_seg[s] - suffix[s * N_HEADS:(s + 1) * N_HEADS] - c)
        total = jnp.sum(sp_seg[s], axis=1, keepdims=True)
        carry = carry + jnp.broadcast_to(total, carry.shape)
    carry_ref[...] = carry
    return jnp.concatenate(a_seg, axis=1)


def _decode_values(vpages, a_ref, acc_ref, fresh):
    n_pages, _, page_tokens = vpages.shape
    for h in range(N_HEADS):
        rows = slice(h * HEAD_DIM, (h + 1) * HEAD_DIM)
        acc = jnp.where(fresh, 0.0, acc_ref[rows, :])
        for p in range(n_pages):
            a_row = a_ref[h:h + 1, p * page_tokens:(p + 1) * page_tokens]
            acc = acc + vpages[p, rows, :] * a_row
        acc_ref[rows, :] = acc


def _attn_kernel(pt_ref, b_ref, q_ref, k_ref, v_ref, tri_ref, ya_ref, sga_ref,
                 x_ref, mod_ref, gattn_ref, wout_ref, gpost_ref,
                 qcol_hbm, bcol_ref, ck_hbm, cv_hbm,
                 y_ref, os_ref,
                 qm_scr, z_scr, a_scr, acc_scr, carry_scr, o_scr,
                 kbuf, vbuf, qbuf, ksem, vsem, qsem, da_scr, dacc_scr, dcarry_scr,
                 chunk_ref, *,
                 extra_chunks):
    blk = ATT_BLOCK
    halves = q_ref.shape[2] // blk
    chains = range(HEADS_PER_GROUP * halves)
    i = pl.program_id(1)
    lane = lax.broadcasted_iota(jnp.int32, (blk, LANES), 1)
    row = lax.broadcasted_iota(jnp.int32, (blk, blk), 0)
    col = lax.broadcasted_iota(jnp.int32, (blk, blk), 1)
    causal = col < row
    tri2 = tri_ref[...]

    n_seq, n_pages = pt_ref.shape
    groups = n_pages // PAGES_PER_CHUNK
    n_chunks = n_seq * groups

    def chunk_copies(c):
        kslot = c % kbuf.shape[0]
        vslot = c % vbuf.shape[0]
        seq = c // groups
        first = (groups - 1 - c % groups) * PAGES_PER_CHUNK
        out = []
        for p in range(PAGES_PER_CHUNK):
            page = pt_ref[seq, first + p]
            out.append(pltpu.make_async_copy(ck_hbm.at[page], kbuf.at[kslot, p],
                                             ksem.at[kslot]))
            out.append(pltpu.make_async_copy(cv_hbm.at[page], vbuf.at[vslot, p],
                                             vsem.at[vslot]))
        return out

    def query_copy(c):
        seq = c // groups
        return pltpu.make_async_copy(qcol_hbm.at[seq], qbuf.at[seq % 2],
                                     qsem.at[seq % 2])

    def start_chunk(c):
        for cp in chunk_copies(c):
            cp.start()

        @pl.when(c % groups == 0)
        def _():
            query_copy(c).start()

    def wait_chunk(c):
        for cp in chunk_copies(c):
            cp.wait()

        @pl.when(c % groups == 0)
        def _():
            query_copy(c).wait()

    def decode_begin():
        c = chunk_ref[0]

        @pl.when(c + 1 < n_chunks)
        def _():
            start_chunk(c + 1)

        wait_chunk(c)
        return c

    def decode_values(c):
        _decode_values(vbuf.at[(c + vbuf.shape[0] - 1) % vbuf.shape[0]], da_scr,
                       dacc_scr, (c + groups - 1) % groups == 0)

    def finish_sequence(c):
        @pl.when((c % groups == 0) & (c > 0))
        def _():
            os_ref[pl.ds((c - 1) // groups, 1), :] = jnp.sum(dacc_scr[...].T, axis=0,
                                                             keepdims=True)

    def decode_keys(c):
        da_scr[...] = _decode_keys(kbuf.at[c % kbuf.shape[0]], qbuf.at[(c // groups) % 2],
                                   bcol_ref[...], tri2, dcarry_scr, c % groups == 0)

    def decode_compute(c):
        decode_values(c)
        decode_keys(c)

    def decode_end(c):
        finish_sequence(c)
        chunk_ref[0] = c + 1

    def decode_chunk():
        c = decode_begin()
        decode_compute(c)
        decode_end(c)

    first_step = (pl.program_id(0) == 0) & (i == 0)
    last_step = ((pl.program_id(0) == pl.num_programs(0) - 1)
                 & (i == pl.num_programs(1) - 1))

    @pl.when(first_step)
    def _():
        chunk_ref[0] = 0
        dacc_scr[...] = jnp.zeros_like(dacc_scr)
        dcarry_scr[...] = jnp.zeros_like(dcarry_scr)
        da_scr[...] = jnp.zeros_like(da_scr)
        vbuf[vbuf.shape[0] - 1] = jnp.zeros(vbuf.shape[1:], F32)
        start_chunk(0)

    def raw_scores(g, key_blk, active):
        off = pl.multiple_of(key_blk * blk, blk)
        kb = k_ref[0, g, pl.ds(off, blk), :]
        qs = jnp.concatenate([qm_scr[c] for c in active], axis=0)
        return lax.dot_general(qs, kb, (((1,), (1,)), ((), ())),
                               preferred_element_type=F32)

    def weights(g, s, active, masked):
        z, sp = [], []
        for n, c in enumerate(active):
            bias2 = b_ref[g * HEADS_PER_GROUP + c // halves] * LOG2E
            zc = s[n * blk:(n + 1) * blk] * LOG2E + bias2
            spc = _softplus2(zc)
            if c in masked:
                spc = jnp.where(causal, spc, 0.0)
            z.append(zc)
            sp.append(spc)
        suffix = _suffix_sum(jnp.concatenate(sp, axis=0), tri2)
        a = []
        for n, c in enumerate(active):
            carry = carry_scr[c]
            c2 = jnp.concatenate([carry] * (blk // LANES), axis=1)
            ac = jnp.exp2(z[n] - suffix[n * blk:(n + 1) * blk] - c2)
            if c in masked:
                ac = jnp.where(causal, ac, 0.0)
            a.append(ac.astype(BF16))
            total = jnp.sum(sp[n], axis=1, keepdims=True)
            carry_scr[c] = carry + jnp.broadcast_to(total, (blk, LANES))
        return jnp.concatenate(a, axis=0)

    def accumulate(g, key_blk, a, active):
        off = pl.multiple_of(key_blk * blk, blk)
        vb = v_ref[0, g, pl.ds(off, blk), :]
        o = jnp.dot(a, vb, preferred_element_type=F32)
        for n, c in enumerate(active):
            acc_scr[c] += o[n * blk:(n + 1) * blk]

    def group_body(g, _):
        qg = q_ref[0, g]
        for c in chains:
            hh, half = divmod(c, halves)
            in_head = (lane >= hh * HEAD_DIM) & (lane < (hh + 1) * HEAD_DIM)
            qh = qg[half * blk:(half + 1) * blk]
            qm_scr[c] = jnp.where(in_head, qh, jnp.zeros_like(qh))
        acc_scr[...] = jnp.zeros_like(acc_scr)
        carry_scr[...] = jnp.zeros_like(carry_scr)

        top = halves * i
        everyone = list(chains)
        z_scr[0] = raw_scores(g, jnp.maximum(top - 1, 0), everyone)

        for d in reversed(range(1, halves)):
            active = [c for c in chains if c % halves >= d]
            masked = {c for c in active if c % halves == d}
            a = weights(g, raw_scores(g, top + d, active), active, masked)
            accumulate(g, top + d, a, active)
        a_scr[1] = weights(g, raw_scores(g, top, everyone), everyone,
                           {c for c in chains if c % halves == 0})

        @pl.when(g < extra_chunks)
        def _():
            decode_chunk()

        assert halves == 2

        def key_body(m, _):
            nxt = top - 1 - halves * m
            chunk = decode_begin()
            decode_parts = [decode_values, decode_keys]
            for cur in range(halves):
                accumulate(g, nxt - cur + 1, a_scr[1 - cur], everyone)
                z_scr[1 - cur] = raw_scores(g, jnp.maximum(nxt - cur - 1, 0), everyone)
                a_scr[cur] = weights(g, z_scr[cur], everyone, set())
                decode_parts[cur](chunk)
            decode_end(chunk)
            return 0

        lax.fori_loop(0, i, key_body, 0)
        accumulate(g, 0, a_scr[1], everyone)
        heads = [jnp.concatenate([acc_scr[hh * halves + half] for half in range(halves)],
                                 axis=0) for hh in range(HEADS_PER_GROUP)]
        lane_q = lax.broadcasted_iota(jnp.int32, heads[0].shape, 1)
        o_scr[g] = jnp.where(lane_q < HEAD_DIM, heads[0], heads[1])
        return 0

    lax.fori_loop(0, N_GROUPS, group_body, 0)

    o = jnp.concatenate([o_scr[g] for g in range(N_GROUPS)], axis=1)
    y_ref[0] = _mixer_out(o, sga_ref[0], ya_ref[0], x_ref[0],
                          _split_mod(mod_ref[0])[2], gattn_ref[...], wout_ref,
                          gpost_ref[...])

    @pl.when(last_step)
    def _():
        def drain(_, carry):
            decode_chunk()
            return carry

        lax.fori_loop(chunk_ref[0], n_chunks, drain, 0)
        decode_values(n_chunks)
        finish_sequence(n_chunks)


def _attention(page_table, b_sb, q4, k4, v4, tri2, ya, sga, x, mod, g_attn, w_out_b,
               g_post, qcol, bias_col, cache_k, cache_v):
    B, S, D = x.shape
    n_seq, n_pages = page_table.shape
    page_tokens = cache_k.shape[2]
    assert n_pages % PAGES_PER_CHUNK == 0 and KEY_SEGMENT % page_tokens == 0
    blk = ATT_QUERY_TILE
    n_chain = HEADS_PER_GROUP * (blk // ATT_BLOCK)
    cdim = ya.shape[-1]
    n_tiles = S // blk
    n_chunks = n_seq * (n_pages // PAGES_PER_CHUNK)
    trips = B * N_GROUPS * sum(range(n_tiles))
    assert trips <= n_chunks, "more key-loop trips than decode chunks"
    extra_chunks = min(N_GROUPS, (n_chunks - trips) // (B * n_tiles))
    row_blk = lambda w: pl.BlockSpec((1, blk, w), lambda b, i, pt: (b, i, 0))
    once = pl.Buffered(1)
    const = lambda shp: pl.BlockSpec(shp, lambda b, i, pt: (0,) * len(shp),
                                     pipeline_mode=once)
    whole_seq = pl.BlockSpec((1, N_GROUPS, S, LANES), lambda b, i, pt: (b, 0, 0, 0),
                             pipeline_mode=once)
    page_buf = lambda slots: pltpu.VMEM(
        (slots, PAGES_PER_CHUNK, ATT_DIM, page_tokens), F32)
    grid_spec = pltpu.PrefetchScalarGridSpec(
        num_scalar_prefetch=1,
        grid=(B, S // blk),
        in_specs=[pl.BlockSpec(memory_space=pltpu.SMEM),
                  pl.BlockSpec((1, N_GROUPS, blk, LANES), lambda b, i, pt: (b, 0, i, 0)),
                  whole_seq, whole_seq, const(tri2.shape),
                  row_blk(cdim), row_blk(ATT_DIM), row_blk(D),
                  pl.BlockSpec((1, 1, 3 * D), lambda b, i, pt: (b, 0, 0)),
                  const((1, ATT_DIM)), const(w_out_b.shape), const((1, D)),
                  pl.BlockSpec(memory_space=pl.ANY),
                  const(bias_col.shape),
                  pl.BlockSpec(memory_space=pl.ANY),
                  pl.BlockSpec(memory_space=pl.ANY)],
        out_specs=[row_blk(D),
                   pl.BlockSpec((n_seq, ATT_DIM), lambda b, i, pt: (0, 0))],
        scratch_shapes=[pltpu.VMEM((n_chain, ATT_BLOCK, LANES), BF16),
                        pltpu.VMEM((2, n_chain * ATT_BLOCK, ATT_BLOCK), F32),
                        pltpu.VMEM((2, n_chain * ATT_BLOCK, ATT_BLOCK), BF16),
                        pltpu.VMEM((n_chain, ATT_BLOCK, LANES), F32),
                        pltpu.VMEM((n_chain, ATT_BLOCK, LANES), F32),
                        pltpu.VMEM((N_GROUPS, blk, LANES), F32),
                        page_buf(2), page_buf(3),
                        pltpu.VMEM((2, ATT_DIM, LANES), F32),
                        pltpu.SemaphoreType.DMA((2,)),
                        pltpu.SemaphoreType.DMA((3,)),
                        pltpu.SemaphoreType.DMA((2,)),
                        pltpu.VMEM((N_HEADS, PAGES_PER_CHUNK * page_tokens), F32),
                        pltpu.VMEM((ATT_DIM, LANES), F32),
                        pltpu.VMEM((N_HEADS, LANES), F32),
                        pltpu.SMEM((1,), jnp.int32)],
    )
    return pl.pallas_call(
        functools.partial(_attn_kernel, extra_chunks=extra_chunks),
        grid_spec=grid_spec,
        out_shape=[jax.ShapeDtypeStruct((B, S, D), F32),
                   jax.ShapeDtypeStruct((n_seq, ATT_DIM), F32)],
        compiler_params=pltpu.CompilerParams(
            dimension_semantics=("arbitrary", "arbitrary"),
            vmem_limit_bytes=VMEM_LIMIT),
        name="attention",
    )(page_table, b_sb, q4, k4, v4, tri2, ya, sga, x, mod, g_attn, w_out_b, g_post,
      qcol, bias_col, cache_k, cache_v)


def _sample_proj_kernel(x_ref, mod_ref, st_ref, gpre_ref, w_ref, cw_ref,
                        gconv_ref, ya_ref, q_ref, k_ref, v_ref, ga_ref, conv_ref):
    cdim = cw_ref.shape[1]
    h = _modulated_norm(x_ref[...], mod_ref[...], gpre_ref[...])
    p = jnp.dot(h, w_ref[...], preferred_element_type=F32)
    seg = lambda i: p[:, i * cdim:(i + 1) * cdim]
    u = seg(2) * seg(0)
    u2, u1 = st_ref[:, 0:cdim], st_ref[:, cdim:2 * cdim]
    ya_ref[...] = _conv_branch(u, u1, u2, seg(1), seg(3), cw_ref[...],
                               gconv_ref[...]).astype(BF16)
    conv_ref[:, 0:cdim] = u1
    conv_ref[:, cdim:2 * cdim] = u
    q_ref[...] = seg(4) * (HEAD_DIM ** -0.5)
    k_ref[...] = seg(5)
    v_ref[...] = seg(6)
    ga = seg(7)
    ga_ref[...] = ga * jax.nn.sigmoid(ga)


def _sample_proj(x, mod, state, g_pre, w_in_b, conv_w, g_conv):
    n, _ = x.shape
    cdim = conv_w.shape[1]
    sds = lambda w, dt=F32: jax.ShapeDtypeStruct((n, w), dt)
    return pl.pallas_call(
        _sample_proj_kernel,
        out_shape=[sds(cdim, BF16), sds(ATT_DIM), sds(ATT_DIM), sds(ATT_DIM),
                   sds(ATT_DIM), sds((CONV_WIDTH - 1) * cdim)],
        compiler_params=pltpu.CompilerParams(vmem_limit_bytes=VMEM_LIMIT),
        name="sample_proj",
    )(x, mod, state, g_pre, w_in_b, conv_w, g_conv)


def _sample_out_kernel(o_ref, sga_ref, ya_ref, x_ref, mod_ref, gattn_ref,
                       wout_ref, gpost_ref, y_ref):
    y_ref[...] = _mixer_out(o_ref[...], sga_ref[...], ya_ref[...], x_ref[...],
                            _split_mod(mod_ref[...])[2], gattn_ref[...], wout_ref,
                            gpost_ref[...])


def _sample_out(o, sga, ya, x, mod, g_attn, w_out_b, g_post):
    return pl.pallas_call(
        _sample_out_kernel,
        out_shape=jax.ShapeDtypeStruct(x.shape, F32),
        compiler_params=pltpu.CompilerParams(vmem_limit_bytes=VMEM_LIMIT),
        name="sample_out",
    )(o, sga, ya, x, mod, g_attn, w_out_b, g_post)


def _layer(xp, xs, c_all, pool_k, pool_v, state, page_table, w_ada, b_ada, g_pre,
           w_in, conv_w, g_conv, b_sb, g_attn, w_out, g_post):
    B, S, D = xp.shape
    n_seq = xs.shape[0]
    row = lambda v: v.reshape(1, -1)
    w_in_b = w_in.astype(BF16)
    w_out_b = w_out.astype(BF16)

    mod = _ada(c_all, w_ada, row(b_ada))
    mod_p, mod_s = mod[:B].reshape(B, 1, 3 * D), mod[B:B + n_seq]

    ya, q4, k, kb4, v, vb4, sga, conv_p = _prompt_proj(
        xp, mod_p, row(g_pre), w_in_b, conv_w, row(g_conv))
    xs2 = xs.reshape(n_seq, D)
    ya_s, q_s, k_s, v_s, sga_s, conv_s = _sample_proj(
        xs2, mod_s, state.reshape(n_seq, -1), row(g_pre), w_in_b, conv_w,
        row(g_conv))
    conv_s = conv_s.reshape(state.shape)

    idx = jnp.arange(ATT_BLOCK)
    tri = (idx[:, None] >= idx[None, :]).astype(BF16)
    tri2 = jnp.concatenate([tri, tri], axis=0)
    n_pool, page_tokens = pool_k.shape[:2]
    assert page_tokens == LANES
    pages = lambda pool: jnp.transpose(pool, (0, 2, 3, 1)).reshape(
        n_pool, ATT_DIM, page_tokens)
    qcol = jnp.broadcast_to(q_s[:, :, None], (n_seq, ATT_DIM, LANES))
    bias_col = jnp.broadcast_to(b_sb[:, None], (N_HEADS, LANES))
    yp, o_s = _attention(page_table, b_sb, q4, kb4, vb4, tri2, ya, sga, xp, mod_p,
                         row(g_attn), w_out_b, row(g_post), qcol, bias_col,
                         pages(pool_k), pages(pool_v))
    ys = _sample_out(o_s, sga_s, ya_s, xs2, mod_s, row(g_attn), w_out_b, row(g_post))

    heads = lambda t, n, L: t.reshape(n, L, N_HEADS, HEAD_DIM)
    return (yp, ys.reshape(xs.shape), heads(k, B, S), heads(v, B, S), conv_p,
            heads(k_s, n_seq, 1), heads(v_s, n_seq, 1), conv_s)


def kernel(x_prompt, x_sample, c_prompt, c_sample, cache_k, cache_v, state_conv,
           page_table, w_ada, b_ada, g_pre, w_in, conv_w, g_conv, b_sb, g_attn,
           w_out, g_post):
    assert x_sample.shape[1] == 1, "decode step handles one new token per sequence"
    depth = w_in.shape[0]
    n_rows = c_prompt.shape[0] + c_sample.shape[0]
    pad = (-n_rows) % 8
    c_all = jnp.concatenate(
        [c_prompt, c_sample, jnp.zeros((pad, c_prompt.shape[1]), F32)], axis=0)
    xp, xs = x_prompt, x_sample
    outs = [[] for _ in range(6)]
    for l in range(depth):
        xp, xs, kp, vp, cp, ksn, vsn, csn = _layer(
            xp, xs, c_all, cache_k[l], cache_v[l], state_conv[l], page_table,
            w_ada[l], b_ada[l], g_pre[l], w_in[l], conv_w[l], g_conv[l], b_sb[l],
            g_attn[l], w_out[l], g_post[l])
        for lst, t in zip(outs, (kp, vp, cp, ksn, vsn, csn)):
            lst.append(t)
    return (xp, xs) + tuple(jnp.stack(t) for t in outs)
```

```python
import functools

import jax
import jax.numpy as jnp
from jax import lax
from jax.experimental import pallas as pl
from jax.experimental.pallas import tpu as pltpu

N_HEADS = 8
HEAD_DIM = 64
ATT_DIM = N_HEADS * HEAD_DIM
CONV_WIDTH = 3
EPS = 1e-6

LANES = 128
HEADS_PER_GROUP = LANES // HEAD_DIM
N_GROUPS = ATT_DIM // LANES

PROJ_ROWS = 512
ATT_BLOCK = 256
ATT_QUERY_TILE = 2 * ATT_BLOCK
PAGES_PER_CHUNK = 8
CHUNKS_AHEAD = 2
KEY_SEGMENT = ATT_BLOCK
VMEM_LIMIT = 56 * 1024 * 1024

F32 = jnp.float32
BF16 = jnp.bfloat16
LOG2E = 1.4426950408889634
SOFTPLUS2_LINEAR_ABOVE = 64.0


def _rms(x, g):
    ms = jnp.mean(x * x, axis=-1, keepdims=True)
    return x * lax.rsqrt(ms + EPS) * g


def _softplus(z):
    return jnp.maximum(z, 0.0) + jnp.log(1.0 + jnp.exp(-jnp.abs(z)))


def _softplus2(z2):
    return jnp.where(z2 > SOFTPLUS2_LINEAR_ABOVE, z2,
                     jnp.log2(1.0 + jnp.exp2(z2)))


def _split_bf16(x):
    hi = x.astype(BF16)
    lo = (x - hi.astype(F32)).astype(BF16)
    return hi, lo


def _suffix_sum(sp, tri2):
    hi, lo = _split_bf16(sp)
    return jnp.dot(jnp.concatenate([hi, lo], axis=1), tri2,
                   preferred_element_type=F32)


def _ada_kernel(c_ref, w_ref, b_ref, o_ref):
    c = c_ref[...]
    a = c * jax.nn.sigmoid(c)
    o_ref[...] = jnp.dot(a, w_ref[...], preferred_element_type=F32,
                         precision=lax.Precision.HIGHEST) + b_ref[...]


def _ada(c_all, w_ada, b_ada):
    rows = c_all.shape[0]
    return pl.pallas_call(
        _ada_kernel,
        out_shape=jax.ShapeDtypeStruct((rows, w_ada.shape[1]), F32),
        compiler_params=pltpu.CompilerParams(vmem_limit_bytes=VMEM_LIMIT),
        name="ada",
    )(c_all, w_ada, b_ada)


def _split_mod(mod):
    d = mod.shape[-1] // 3
    return mod[:, 0:d], mod[:, d:2 * d], mod[:, 2 * d:3 * d]


def _modulated_norm(x, mod, g_pre):
    shift, scale, _ = _split_mod(mod)
    return (_rms(x, g_pre) * (1.0 + scale) + shift).astype(BF16)


def _conv_branch(u, u1, u2, bc, gc, conv_w, g_conv):
    conv_y = conv_w[0:1] * u2 + conv_w[1:2] * u1 + conv_w[2:3] * u
    return _rms(bc * conv_y, g_conv) * (gc * jax.nn.sigmoid(gc))


def _prompt_proj_kernel(x_ref, mod_ref, gpre_ref, w_ref, cw_ref, gconv_ref,
                        ya_ref, q_ref, k_ref, kb_ref, v_ref, vb_ref, ga_ref,
                        conv_ref, u_scr):
    rows = x_ref.shape[1]
    cdim = cw_ref.shape[1]
    h = _modulated_norm(x_ref[0], mod_ref[0], gpre_ref[...])

    def seg(i):
        return jnp.dot(h, w_ref[:, i * cdim:(i + 1) * cdim],
                       preferred_element_type=F32)

    @pl.when(pl.program_id(1) == 0)
    def _():
        u_scr[0:8, :] = jnp.zeros((8, cdim), F32)

    u = seg(2) * seg(0)
    u_scr[8:8 + rows, :] = u
    u1 = u_scr[7:7 + rows, :]
    u2 = u_scr[6:6 + rows, :]
    ya = _conv_branch(u, u1, u2, seg(1), seg(3), cw_ref[...], gconv_ref[...])
    ya_ref[0] = ya.astype(BF16)
    tail = u_scr[rows + 6:rows + 8, :]
    conv_ref[0] = tail
    u_scr[6:8, :] = tail

    q = (seg(4) * (LOG2E * HEAD_DIM ** -0.5)).astype(BF16)
    k = seg(5)
    v = seg(6)
    k_ref[0] = k
    v_ref[0] = v
    kb = k.astype(BF16)
    vb = v.astype(BF16)
    for g in range(N_GROUPS):
        sl = slice(g * LANES, (g + 1) * LANES)
        q_ref[0, g] = q[:, sl]
        kb_ref[0, g] = kb[:, sl]
        vb_ref[0, g] = vb[:, sl]
    ga = seg(7)
    ga_ref[0] = ga * jax.nn.sigmoid(ga)


def _prompt_proj(x, mod, g_pre, w_in_b, conv_w, g_conv):
    B, S, D = x.shape
    cdim = conv_w.shape[1]
    rows = PROJ_ROWS
    grid = (B, S // rows)
    row_blk = lambda w: pl.BlockSpec((1, rows, w), lambda b, s: (b, s, 0))
    grp_blk = pl.BlockSpec((1, N_GROUPS, rows, LANES), lambda b, s: (b, 0, s, 0))
    const = lambda shp: pl.BlockSpec(shp, lambda b, s: (0,) * len(shp))
    grp_shape = jax.ShapeDtypeStruct((B, N_GROUPS, S, LANES), BF16)
    return pl.pallas_call(
        _prompt_proj_kernel,
        grid=grid,
        in_specs=[row_blk(D),
                  pl.BlockSpec((1, 1, 3 * D), lambda b, s: (b, 0, 0)),
                  const((1, D)), const(w_in_b.shape), const(conv_w.shape),
                  const((1, cdim))],
        out_specs=[row_blk(cdim), grp_blk, row_blk(ATT_DIM), grp_blk,
                   row_blk(ATT_DIM), grp_blk, row_blk(ATT_DIM),
                   pl.BlockSpec((1, CONV_WIDTH - 1, cdim), lambda b, s: (b, 0, 0))],
        out_shape=[jax.ShapeDtypeStruct((B, S, cdim), BF16), grp_shape,
                   jax.ShapeDtypeStruct((B, S, ATT_DIM), F32), grp_shape,
                   jax.ShapeDtypeStruct((B, S, ATT_DIM), F32), grp_shape,
                   jax.ShapeDtypeStruct((B, S, ATT_DIM), F32),
                   jax.ShapeDtypeStruct((B, CONV_WIDTH - 1, cdim), F32)],
        scratch_shapes=[pltpu.VMEM((rows + 8, cdim), F32)],
        compiler_params=pltpu.CompilerParams(
            dimension_semantics=("arbitrary", "arbitrary"),
            vmem_limit_bytes=VMEM_LIMIT),
        name="prompt_proj",
    )(x, mod, g_pre, w_in_b, conv_w, g_conv)


def _mixer_out(o, sga, ya, x, gate, g_attn, w_out_ref, g_post):
    cdim = ya.shape[-1]
    yb = (_rms(o, g_attn) * sga).astype(BF16)
    m = (jnp.dot(ya, w_out_ref[0:cdim, :], preferred_element_type=F32)
         + jnp.dot(yb, w_out_ref[cdim:, :], preferred_element_type=F32))
    return x + gate * _rms(m, g_post)


def _decode_keys(kpages, qcol_ref, bias, tri2, carry_ref, fresh):
    n_pages, _, page_tokens = kpages.shape
    seg_pages = KEY_SEGMENT // page_tokens
    n_seg = n_pages // seg_pages
    sub = 8

    z_rows = [[None] * N_HEADS for _ in range(n_pages)]
    for h in range(N_HEADS):
        part = [None] * n_pages
        for r in range(HEAD_DIM // sub):
            rows = slice(h * HEAD_DIM + r * sub, h * HEAD_DIM + (r + 1) * sub)
            q = qcol_ref[rows, :]
            for p in range(n_pages):
                t = kpages[p, rows, :] * q
                part[p] = t if r == 0 else part[p] + t
        for p in range(n_pages):
            z_rows[p][h] = jnp.sum(part[p], axis=0, keepdims=True)
    z_seg, sp_seg = [], []
    for s in range(n_seg):
        z = jnp.concatenate(
            [jnp.concatenate(z_rows[p], axis=0)
             for p in range(s * seg_pages, (s + 1) * seg_pages)], axis=1)
        z = z + jnp.concatenate([bias] * seg_pages, axis=1)
        z_seg.append(z)
        sp_seg.append(_softplus(z))

    suffix = _suffix_sum(jnp.concatenate(sp_seg, axis=0), tri2)

    carry = jnp.where(fresh, 0.0, carry_ref[...])
    a_seg = [None] * n_seg
    for s in reversed(range(n_seg)):
        c = jnp.concatenate([carry] * seg_pages, axis=1)
        a_seg[s] = jnp.exp(z_seg[s] - suffix[s * N_HEADS:(s + 1) * N_HEADS] - c)
        total = jnp.sum(sp_seg[s], axis=1, keepdims=True)
        carry = carry + jnp.broadcast_to(total, carry.shape)
    carry_ref[...] = carry
    return jnp.concatenate(a_seg, axis=1)


def _decode_values(vpages, a_ref, acc_ref, fresh):
    n_pages, _, page_tokens = vpages.shape
    for h in range(N_HEADS):
        rows = slice(h * HEAD_DIM, (h + 1) * HEAD_DIM)
        acc = jnp.where(fresh, 0.0, acc_ref[rows, :])
        for p in range(n_pages):
            a_row = a_ref[h:h + 1, p * page_tokens:(p + 1) * page_tokens]
            acc = acc + vpages[p, rows, :] * a_row
        acc_ref[rows, :] = acc


def _attn_kernel(pt_ref, b_ref, q_ref, k_ref, v_ref, tri_ref, ya_ref, sga_ref,
                 x_ref, mod_ref, gattn_ref, wout_ref, gpost_ref,
                 qcol_hbm, bcol_ref, ck_hbm, cv_hbm,
                 y_ref, os_ref,
                 qm_scr, z_scr, a_scr, acc_scr, carry_scr, o_scr,
                 kbuf, vbuf, qbuf, ksem, vsem, qsem, da_scr, dacc_scr, dcarry_scr,
                 chunk_ref, *,
                 extra_chunks):
    blk = ATT_BLOCK
    halves = q_ref.shape[2] // blk
    chains = range(HEADS_PER_GROUP * halves)
    i = pl.program_id(1)
    lane = lax.broadcasted_iota(jnp.int32, (blk, LANES), 1)
    row = lax.broadcasted_iota(jnp.int32, (blk, blk), 0)
    col = lax.broadcasted_iota(jnp.int32, (blk, blk), 1)
    causal = col < row
    tri2 = tri_ref[...]

    n_seq, n_pages = pt_ref.shape
    groups = n_pages // PAGES_PER_CHUNK
    n_chunks = n_seq * groups

    def chunk_copies(c):
        kslot = c % kbuf.shape[0]
        vslot = c % vbuf.shape[0]
        seq = c // groups
        first = (groups - 1 - c % groups) * PAGES_PER_CHUNK
        out = []
        for p in range(PAGES_PER_CHUNK):
            page = pt_ref[seq, first + p]
            out.append(pltpu.make_async_copy(ck_hbm.at[page], kbuf.at[kslot, p],
                                             ksem.at[kslot]))
            out.append(pltpu.make_async_copy(cv_hbm.at[page], vbuf.at[vslot, p],
                                             vsem.at[vslot]))
        return out

    def query_copy(c):
        seq = c // groups
        return pltpu.make_async_copy(qcol_hbm.at[seq], qbuf.at[seq % 2],
                                     qsem.at[seq % 2])

    def start_chunk(c):
        for cp in chunk_copies(c):
            cp.start()

        @pl.when(c % groups == 0)
        def _():
            query_copy(c).start()

    def wait_chunk(c):
        for cp in chunk_copies(c):
            cp.wait()

        @pl.when(c % groups == 0)
        def _():
            query_copy(c).wait()

    def decode_begin():
        c = chunk_ref[0]

        @pl.when(c + CHUNKS_AHEAD < n_chunks)
        def _():
            start_chunk(c + CHUNKS_AHEAD)

        wait_chunk(c)
        return c

    def decode_values(c):
        _decode_values(vbuf.at[(c + vbuf.shape[0] - 1) % vbuf.shape[0]], da_scr,
                       dacc_scr, (c + groups - 1) % groups == 0)

    def finish_sequence(c):
        @pl.when((c % groups == 0) & (c > 0))
        def _():
            os_ref[pl.ds((c - 1) // groups, 1), :] = jnp.sum(dacc_scr[...].T, axis=0,
                                                             keepdims=True)

    def decode_keys(c):
        da_scr[...] = _decode_keys(kbuf.at[c % kbuf.shape[0]], qbuf.at[(c // groups) % 2],
                                   bcol_ref[...], tri2, dcarry_scr, c % groups == 0)

    def decode_compute(c):
        decode_values(c)
        decode_keys(c)

    def decode_end(c):
        finish_sequence(c)
        chunk_ref[0] = c + 1

    def decode_chunk():
        c = decode_begin()
        decode_compute(c)
        decode_end(c)

    first_step = (pl.program_id(0) == 0) & (i == 0)
    last_step = ((pl.program_id(0) == pl.num_programs(0) - 1)
                 & (i == pl.num_programs(1) - 1))

    @pl.when(first_step)
    def _():
        chunk_ref[0] = 0
        dacc_scr[...] = jnp.zeros_like(dacc_scr)
        dcarry_scr[...] = jnp.zeros_like(dcarry_scr)
        da_scr[...] = jnp.zeros_like(da_scr)
        vbuf[vbuf.shape[0] - 1] = jnp.zeros(vbuf.shape[1:], F32)
        for c in range(CHUNKS_AHEAD):
            start_chunk(jnp.int32(c))

    def raw_scores(g, key_blk, active):
        off = pl.multiple_of(key_blk * blk, blk)
        kb = k_ref[0, g, pl.ds(off, blk), :]
        qs = jnp.concatenate([qm_scr[c] for c in active], axis=0)
        return lax.dot_general(qs, kb, (((1,), (1,)), ((), ())),
                               preferred_element_type=F32)

    def weights(g, s, active, masked):
        z, sp = [], []
        for n, c in enumerate(active):
            bias2 = b_ref[g * HEADS_PER_GROUP + c // halves] * LOG2E
            zc = s[n * blk:(n + 1) * blk] + bias2
            spc = _softplus2(zc)
            if c in masked:
                spc = jnp.where(causal, spc, 0.0)
            z.append(zc)
            sp.append(spc)
        suffix = jnp.dot(jnp.concatenate(sp, axis=0).astype(BF16), tri2[0:blk],
                         preferred_element_type=F32)
        a = []
        for n, c in enumerate(active):
            carry = carry_scr[c]
            c2 = jnp.concatenate([carry] * (blk // LANES), axis=1)
            ac = jnp.exp2(z[n] - suffix[n * blk:(n + 1) * blk] - c2)
            if c in masked:
                ac = jnp.where(causal, ac, 0.0)
            a.append(ac.astype(BF16))
            total = jnp.sum(sp[n], axis=1, keepdims=True)
            carry_scr[c] = carry + jnp.broadcast_to(total, (blk, LANES))
        return jnp.concatenate(a, axis=0)

    def accumulate(g, key_blk, a, active):
        off = pl.multiple_of(key_blk * blk, blk)
        vb = v_ref[0, g, pl.ds(off, blk), :]
        o = jnp.dot(a, vb, preferred_element_type=F32)
        for n, c in enumerate(active):
            acc_scr[c] += o[n * blk:(n + 1) * blk]

    def group_body(g, _):
        qg = q_ref[0, g]
        for c in chains:
            hh, half = divmod(c, halves)
            in_head = (lane >= hh * HEAD_DIM) & (lane < (hh + 1) * HEAD_DIM)
            qh = qg[half * blk:(half + 1) * blk]
            qm_scr[c] = jnp.where(in_head, qh, jnp.zeros_like(qh))
        acc_scr[...] = jnp.zeros_like(acc_scr)
        carry_scr[...] = jnp.zeros_like(carry_scr)

        top = halves * i
        everyone = list(chains)
        z_scr[0] = raw_scores(g, jnp.maximum(top - 1, 0), everyone)

        for d in reversed(range(1, halves)):
            active = [c for c in chains if c % halves >= d]
            masked = {c for c in active if c % halves == d}
            a = weights(g, raw_scores(g, top + d, active), active, masked)
            accumulate(g, top + d, a, active)
        a_scr[1] = weights(g, raw_scores(g, top, everyone), everyone,
                           {c for c in chains if c % halves == 0})

        @pl.when(g < extra_chunks)
        def _():
            decode_chunk()

        assert halves == 2

        def key_body(m, _):
            nxt = top - 1 - halves * m
            chunk = decode_begin()
            decode_parts = [decode_values, decode_keys]
            for cur in range(halves):
                accumulate(g, nxt - cur + 1, a_scr[1 - cur], everyone)
                z_scr[1 - cur] = raw_scores(g, jnp.maximum(nxt - cur - 1, 0), everyone)
                a_scr[cur] = weights(g, z_scr[cur], everyone, set())
                decode_parts[cur](chunk)
            decode_end(chunk)
            return 0

        lax.fori_loop(0, i, key_body, 0)
        accumulate(g, 0, a_scr[1], everyone)
        heads = [jnp.concatenate([acc_scr[hh * halves + half] for half in range(halves)],
                                 axis=0) for hh in range(HEADS_PER_GROUP)]
        lane_q = lax.broadcasted_iota(jnp.int32, heads[0].shape, 1)
        o_scr[g] = jnp.where(lane_q < HEAD_DIM, heads[0], heads[1])
        return 0

    lax.fori_loop(0, N_GROUPS, group_body, 0)

    o = jnp.concatenate([o_scr[g] for g in range(N_GROUPS)], axis=1)
    y_ref[0] = _mixer_out(o, sga_ref[0], ya_ref[0], x_ref[0],
                          _split_mod(mod_ref[0])[2], gattn_ref[...], wout_ref,
                          gpost_ref[...])

    @pl.when(last_step)
    def _():
        def drain(_, carry):
            decode_chunk()
            return carry

        lax.fori_loop(chunk_ref[0], n_chunks, drain, 0)
        decode_values(n_chunks)
        finish_sequence(n_chunks)


def _attention(page_table, b_sb, q4, k4, v4, tri2, ya, sga, x, mod, g_attn, w_out_b,
               g_post, qcol, bias_col, cache_k, cache_v):
    B, S, D = x.shape
    n_seq, n_pages = page_table.shape
    page_tokens = cache_k.shape[2]
    assert n_pages % PAGES_PER_CHUNK == 0 and KEY_SEGMENT % page_tokens == 0
    blk = ATT_QUERY_TILE
    n_chain = HEADS_PER_GROUP * (blk // ATT_BLOCK)
    cdim = ya.shape[-1]
    n_tiles = S // blk
    n_chunks = n_seq * (n_pages // PAGES_PER_CHUNK)
    trips = B * N_GROUPS * sum(range(n_tiles))
    assert trips <= n_chunks, "more key-loop trips than decode chunks"
    extra_chunks = min(N_GROUPS, (n_chunks - trips) // (B * n_tiles))
    row_blk = lambda w: pl.BlockSpec((1, blk, w), lambda b, i, pt: (b, i, 0))
    once = pl.Buffered(1)
    const = lambda shp: pl.BlockSpec(shp, lambda b, i, pt: (0,) * len(shp),
                                     pipeline_mode=once)
    whole_seq = pl.BlockSpec((1, N_GROUPS, S, LANES), lambda b, i, pt: (b, 0, 0, 0),
                             pipeline_mode=once)
    page_buf = lambda slots: pltpu.VMEM(
        (slots, PAGES_PER_CHUNK, ATT_DIM, page_tokens), F32)
    grid_spec = pltpu.PrefetchScalarGridSpec(
        num_scalar_prefetch=1,
        grid=(B, S // blk),
        in_specs=[pl.BlockSpec(memory_space=pltpu.SMEM),
                  pl.BlockSpec((1, N_GROUPS, blk, LANES), lambda b, i, pt: (b, 0, i, 0)),
                  whole_seq, whole_seq, const(tri2.shape),
                  row_blk(cdim), row_blk(ATT_DIM), row_blk(D),
                  pl.BlockSpec((1, 1, 3 * D), lambda b, i, pt: (b, 0, 0)),
                  const((1, ATT_DIM)), const(w_out_b.shape), const((1, D)),
                  pl.BlockSpec(memory_space=pl.ANY),
                  const(bias_col.shape),
                  pl.BlockSpec(memory_space=pl.ANY),
                  pl.BlockSpec(memory_space=pl.ANY)],
        out_specs=[row_blk(D),
                   pl.BlockSpec((n_seq, ATT_DIM), lambda b, i, pt: (0, 0))],
        scratch_shapes=[pltpu.VMEM((n_chain, ATT_BLOCK, LANES), BF16),
                        pltpu.VMEM((2, n_chain * ATT_BLOCK, ATT_BLOCK), F32),
                        pltpu.VMEM((2, n_chain * ATT_BLOCK, ATT_BLOCK), BF16),
                        pltpu.VMEM((n_chain, ATT_BLOCK, LANES), F32),
                        pltpu.VMEM((n_chain, ATT_BLOCK, LANES), F32),
                        pltpu.VMEM((N_GROUPS, blk, LANES), F32),
                        page_buf(CHUNKS_AHEAD + 1), page_buf(CHUNKS_AHEAD + 2),
                        pltpu.VMEM((2, ATT_DIM, LANES), F32),
                        pltpu.SemaphoreType.DMA((CHUNKS_AHEAD + 1,)),
                        pltpu.SemaphoreType.DMA((CHUNKS_AHEAD + 2,)),
                        pltpu.SemaphoreType.DMA((2,)),
                        pltpu.VMEM((N_HEADS, PAGES_PER_CHUNK * page_tokens), F32),
                        pltpu.VMEM((ATT_DIM, LANES), F32),
                        pltpu.VMEM((N_HEADS, LANES), F32),
                        pltpu.SMEM((1,), jnp.int32)],
    )
    return pl.pallas_call(
        functools.partial(_attn_kernel, extra_chunks=extra_chunks),
        grid_spec=grid_spec,
        out_shape=[jax.ShapeDtypeStruct((B, S, D), F32),
                   jax.ShapeDtypeStruct((n_seq, ATT_DIM), F32)],
        compiler_params=pltpu.CompilerParams(
            dimension_semantics=("arbitrary", "arbitrary"),
            vmem_limit_bytes=VMEM_LIMIT),
        name="attention",
    )(page_table, b_sb, q4, k4, v4, tri2, ya, sga, x, mod, g_attn, w_out_b, g_post,
      qcol, bias_col, cache_k, cache_v)


def _sample_proj_kernel(x_ref, mod_ref, st_ref, gpre_ref, w_ref, cw_ref,
                        gconv_ref, ya_ref, q_ref, k_ref, v_ref, ga_ref, conv_ref):
    cdim = cw_ref.shape[1]
    h = _modulated_norm(x_ref[...], mod_ref[...], gpre_ref[...])
    p = jnp.dot(h, w_ref[...], preferred_element_type=F32)
    seg = lambda i: p[:, i * cdim:(i + 1) * cdim]
    u = seg(2) * seg(0)
    u2, u1 = st_ref[:, 0:cdim], st_ref[:, cdim:2 * cdim]
    ya_ref[...] = _conv_branch(u, u1, u2, seg(1), seg(3), cw_ref[...],
                               gconv_ref[...]).astype(BF16)
    conv_ref[:, 0:cdim] = u1
    conv_ref[:, cdim:2 * cdim] = u
    q_ref[...] = seg(4) * (HEAD_DIM ** -0.5)
    k_ref[...] = seg(5)
    v_ref[...] = seg(6)
    ga = seg(7)
    ga_ref[...] = ga * jax.nn.sigmoid(ga)


def _sample_proj(x, mod, state, g_pre, w_in_b, conv_w, g_conv):
    n, _ = x.shape
    cdim = conv_w.shape[1]
    sds = lambda w, dt=F32: jax.ShapeDtypeStruct((n, w), dt)
    return pl.pallas_call(
        _sample_proj_kernel,
        out_shape=[sds(cdim, BF16), sds(ATT_DIM), sds(ATT_DIM), sds(ATT_DIM),
                   sds(ATT_DIM), sds((CONV_WIDTH - 1) * cdim)],
        compiler_params=pltpu.CompilerParams(vmem_limit_bytes=VMEM_LIMIT),
        name="sample_proj",
    )(x, mod, state, g_pre, w_in_b, conv_w, g_conv)


def _sample_out_kernel(o_ref, sga_ref, ya_ref, x_ref, mod_ref, gattn_ref,
                       wout_ref, gpost_ref, y_ref):
    y_ref[...] = _mixer_out(o_ref[...], sga_ref[...], ya_ref[...], x_ref[...],
                            _split_mod(mod_ref[...])[2], gattn_ref[...], wout_ref,
                            gpost_ref[...])


def _sample_out(o, sga, ya, x, mod, g_attn, w_out_b, g_post):
    return pl.pallas_call(
        _sample_out_kernel,
        out_shape=jax.ShapeDtypeStruct(x.shape, F32),
        compiler_params=pltpu.CompilerParams(vmem_limit_bytes=VMEM_LIMIT),
        name="sample_out",
    )(o, sga, ya, x, mod, g_attn, w_out_b, g_post)


def _layer(xp, xs, c_all, pool_k, pool_v, state, page_table, w_ada, b_ada, g_pre,
           w_in, conv_w, g_conv, b_sb, g_attn, w_out, g_post):
    B, S, D = xp.shape
    n_seq = xs.shape[0]
    row = lambda v: v.reshape(1, -1)
    w_in_b = w_in.astype(BF16)
    w_out_b = w_out.astype(BF16)

    mod = _ada(c_all, w_ada, row(b_ada))
    mod_p, mod_s = mod[:B].reshape(B, 1, 3 * D), mod[B:B + n_seq]

    ya, q4, k, kb4, v, vb4, sga, conv_p = _prompt_proj(
        xp, mod_p, row(g_pre), w_in_b, conv_w, row(g_conv))
    xs2 = xs.reshape(n_seq, D)
    ya_s, q_s, k_s, v_s, sga_s, conv_s = _sample_proj(
        xs2, mod_s, state.reshape(n_seq, -1), row(g_pre), w_in_b, conv_w,
        row(g_conv))
    conv_s = conv_s.reshape(state.shape)

    idx = jnp.arange(ATT_BLOCK)
    tri = (idx[:, None] >= idx[None, :]).astype(BF16)
    tri2 = jnp.concatenate([tri, tri], axis=0)
    n_pool, page_tokens = pool_k.shape[:2]
    assert page_tokens == LANES
    pages = lambda pool: jnp.transpose(pool, (0, 2, 3, 1)).reshape(
        n_pool, ATT_DIM, page_tokens)
    qcol = jnp.broadcast_to(q_s[:, :, None], (n_seq, ATT_DIM, LANES))
    bias_col = jnp.broadcast_to(b_sb[:, None], (N_HEADS, LANES))
    yp, o_s = _attention(page_table, b_sb, q4, kb4, vb4, tri2, ya, sga, xp, mod_p,
                         row(g_attn), w_out_b, row(g_post), qcol, bias_col,
                         pages(pool_k), pages(pool_v))
    ys = _sample_out(o_s, sga_s, ya_s, xs2, mod_s, row(g_attn), w_out_b, row(g_post))

    heads = lambda t, n, L: t.reshape(n, L, N_HEADS, HEAD_DIM)
    return (yp, ys.reshape(xs.shape), heads(k, B, S), heads(v, B, S), conv_p,
            heads(k_s, n_seq, 1), heads(v_s, n_seq, 1), conv_s)


def kernel(x_prompt, x_sample, c_prompt, c_sample, cache_k, cache_v, state_conv,
           page_table, w_ada, b_ada, g_pre, w_in, conv_w, g_conv, b_sb, g_attn,
           w_out, g_post):
    assert x_sample.shape[1] == 1, "decode step handles one new token per sequence"
    depth = w_in.shape[0]
    n_rows = c_prompt.shape[0] + c_sample.shape[0]
    pad = (-n_rows) % 8
    c_all = jnp.concatenate(
        [c_prompt, c_sample, jnp.zeros((pad, c_prompt.shape[1]), F32)], axis=0)
    xp, xs = x_prompt, x_sample
    outs = [[] for _ in range(6)]
    for l in range(depth):
        xp, xs, kp, vp, cp, ksn, vsn, csn = _layer(
            xp, xs, c_all, cache_k[l], cache_v[l], state_conv[l], page_table,
            w_ada[l], b_ada[l], g_pre[l], w_in[l], conv_w[l], g_conv[l], b_sb[l],
            g_attn[l], w_out[l], g_post[l])
        for lst, t in zip(outs, (kp, vp, cp, ksn, vsn, csn)):
            lst.append(t)
    return (xp, xs) + tuple(jnp.stack(t) for t in outs)
```

```python
import functools

import jax
import jax.numpy as jnp
from jax import lax
from jax.experimental import pallas as pl
from jax.experimental.pallas import tpu as pltpu

N_HEADS = 8
HEAD_DIM = 64
ATT_DIM = N_HEADS * HEAD_DIM
CONV_WIDTH = 3
EPS = 1e-6

LANES = 128
HEADS_PER_GROUP = LANES // HEAD_DIM
N_GROUPS = ATT_DIM // LANES

PROJ_ROWS = 512
ATT_BLOCK = 256
ATT_QUERY_TILE = 2 * ATT_BLOCK
GROUPS_PER_BODY = 2
PAGES_PER_CHUNK = 8
CHUNKS_AHEAD = 2
KEY_SEGMENT = ATT_BLOCK
VMEM_LIMIT = 56 * 1024 * 1024

F32 = jnp.float32
BF16 = jnp.bfloat16
LOG2E = 1.4426950408889634
SOFTPLUS2_LINEAR_ABOVE = 64.0


def _rms(x, g):
    ms = jnp.mean(x * x, axis=-1, keepdims=True)
    return x * lax.rsqrt(ms + EPS) * g


def _softplus(z):
    return jnp.maximum(z, 0.0) + jnp.log(1.0 + jnp.exp(-jnp.abs(z)))


def _softplus2(z2):
    return jnp.where(z2 > SOFTPLUS2_LINEAR_ABOVE, z2,
                     jnp.log2(1.0 + jnp.exp2(z2)))


def _split_bf16(x):
    hi = x.astype(BF16)
    lo = (x - hi.astype(F32)).astype(BF16)
    return hi, lo


def _suffix_sum(sp, tri2):
    hi, lo = _split_bf16(sp)
    return jnp.dot(jnp.concatenate([hi, lo], axis=1), tri2,
                   preferred_element_type=F32)


def _ada_kernel(c_ref, w_ref, b_ref, o_ref):
    c = c_ref[...]
    a = c * jax.nn.sigmoid(c)
    o_ref[...] = jnp.dot(a, w_ref[...], preferred_element_type=F32,
                         precision=lax.Precision.HIGHEST) + b_ref[...]


def _ada(c_all, w_ada, b_ada):
    rows = c_all.shape[0]
    return pl.pallas_call(
        _ada_kernel,
        out_shape=jax.ShapeDtypeStruct((rows, w_ada.shape[1]), F32),
        compiler_params=pltpu.CompilerParams(vmem_limit_bytes=VMEM_LIMIT),
        name="ada",
    )(c_all, w_ada, b_ada)


def _split_mod(mod):
    d = mod.shape[-1] // 3
    return mod[:, 0:d], mod[:, d:2 * d], mod[:, 2 * d:3 * d]


def _modulated_norm(x, mod, g_pre):
    shift, scale, _ = _split_mod(mod)
    return (_rms(x, g_pre) * (1.0 + scale) + shift).astype(BF16)


def _conv_branch(u, u1, u2, bc, gc, conv_w, g_conv):
    conv_y = conv_w[0:1] * u2 + conv_w[1:2] * u1 + conv_w[2:3] * u
    return _rms(bc * conv_y, g_conv) * (gc * jax.nn.sigmoid(gc))


def _prompt_proj_kernel(x_ref, mod_ref, gpre_ref, w_ref, cw_ref, gconv_ref,
                        ya_ref, q_ref, k_ref, kb_ref, v_ref, vb_ref, ga_ref,
                        conv_ref, u_scr):
    rows = x_ref.shape[1]
    cdim = cw_ref.shape[1]
    h = _modulated_norm(x_ref[0], mod_ref[0], gpre_ref[...])

    def seg(i):
        return jnp.dot(h, w_ref[:, i * cdim:(i + 1) * cdim],
                       preferred_element_type=F32)

    @pl.when(pl.program_id(1) == 0)
    def _():
        u_scr[0:8, :] = jnp.zeros((8, cdim), F32)

    u = seg(2) * seg(0)
    u_scr[8:8 + rows, :] = u
    u1 = u_scr[7:7 + rows, :]
    u2 = u_scr[6:6 + rows, :]
    ya = _conv_branch(u, u1, u2, seg(1), seg(3), cw_ref[...], gconv_ref[...])
    ya_ref[0] = ya.astype(BF16)
    tail = u_scr[rows + 6:rows + 8, :]
    conv_ref[0] = tail
    u_scr[6:8, :] = tail

    q = (seg(4) * (LOG2E * HEAD_DIM ** -0.5)).astype(BF16)
    k = seg(5)
    v = seg(6)
    k_ref[0] = k
    v_ref[0] = v
    kb = k.astype(BF16)
    vb = v.astype(BF16)
    for g in range(N_GROUPS):
        sl = slice(g * LANES, (g + 1) * LANES)
        q_ref[0, g] = q[:, sl]
        kb_ref[0, g] = kb[:, sl]
        vb_ref[0, g] = vb[:, sl]
    ga = seg(7)
    ga_ref[0] = ga * jax.nn.sigmoid(ga)


def _prompt_proj(x, mod, g_pre, w_in_b, conv_w, g_conv):
    B, S, D = x.shape
    cdim = conv_w.shape[1]
    rows = PROJ_ROWS
    grid = (B, S // rows)
    row_blk = lambda w: pl.BlockSpec((1, rows, w), lambda b, s: (b, s, 0))
    grp_blk = pl.BlockSpec((1, N_GROUPS, rows, LANES), lambda b, s: (b, 0, s, 0))
    const = lambda shp: pl.BlockSpec(shp, lambda b, s: (0,) * len(shp))
    grp_shape = jax.ShapeDtypeStruct((B, N_GROUPS, S, LANES), BF16)
    return pl.pallas_call(
        _prompt_proj_kernel,
        grid=grid,
        in_specs=[row_blk(D),
                  pl.BlockSpec((1, 1, 3 * D), lambda b, s: (b, 0, 0)),
                  const((1, D)), const(w_in_b.shape), const(conv_w.shape),
                  const((1, cdim))],
        out_specs=[row_blk(cdim), grp_blk, row_blk(ATT_DIM), grp_blk,
                   row_blk(ATT_DIM), grp_blk, row_blk(ATT_DIM),
                   pl.BlockSpec((1, CONV_WIDTH - 1, cdim), lambda b, s: (b, 0, 0))],
        out_shape=[jax.ShapeDtypeStruct((B, S, cdim), BF16), grp_shape,
                   jax.ShapeDtypeStruct((B, S, ATT_DIM), F32), grp_shape,
                   jax.ShapeDtypeStruct((B, S, ATT_DIM), F32), grp_shape,
                   jax.ShapeDtypeStruct((B, S, ATT_DIM), F32),
                   jax.ShapeDtypeStruct((B, CONV_WIDTH - 1, cdim), F32)],
        scratch_shapes=[pltpu.VMEM((rows + 8, cdim), F32)],
        compiler_params=pltpu.CompilerParams(
            dimension_semantics=("arbitrary", "arbitrary"),
            vmem_limit_bytes=VMEM_LIMIT),
        name="prompt_proj",
    )(x, mod, g_pre, w_in_b, conv_w, g_conv)


def _mixer_out(o, sga, ya, x, gate, g_attn, w_out_ref, g_post):
    cdim = ya.shape[-1]
    yb = (_rms(o, g_attn) * sga).astype(BF16)
    m = (jnp.dot(ya, w_out_ref[0:cdim, :], preferred_element_type=F32)
         + jnp.dot(yb, w_out_ref[cdim:, :], preferred_element_type=F32))
    return x + gate * _rms(m, g_post)


def _decode_keys(kpages, qcol_ref, bias, tri2, carry_ref, fresh):
    n_pages, _, page_tokens = kpages.shape
    seg_pages = KEY_SEGMENT // page_tokens
    n_seg = n_pages // seg_pages
    sub = 8

    z_rows = [[None] * N_HEADS for _ in range(n_pages)]
    for h in range(N_HEADS):
        part = [None] * n_pages
        for r in range(HEAD_DIM // sub):
            rows = slice(h * HEAD_DIM + r * sub, h * HEAD_DIM + (r + 1) * sub)
            q = qcol_ref[rows, :]
            for p in range(n_pages):
                t = kpages[p, rows, :] * q
                part[p] = t if r == 0 else part[p] + t
        for p in range(n_pages):
            z_rows[p][h] = jnp.sum(part[p], axis=0, keepdims=True)
    z_seg, sp_seg = [], []
    for s in range(n_seg):
        z = jnp.concatenate(
            [jnp.concatenate(z_rows[p], axis=0)
             for p in range(s * seg_pages, (s + 1) * seg_pages)], axis=1)
        z = z + jnp.concatenate([bias] * seg_pages, axis=1)
        z_seg.append(z)
        sp_seg.append(_softplus(z))

    suffix = _suffix_sum(jnp.concatenate(sp_seg, axis=0), tri2)

    carry = jnp.where(fresh, 0.0, carry_ref[...])
    a_seg = [None] * n_seg
    for s in reversed(range(n_seg)):
        c = jnp.concatenate([carry] * seg_pages, axis=1)
        a_seg[s] = jnp.exp(z_seg[s] - suffix[s * N_HEADS:(s + 1) * N_HEADS] - c)
        total = jnp.sum(sp_seg[s], axis=1, keepdims=True)
        carry = carry + jnp.broadcast_to(total, carry.shape)
    carry_ref[...] = carry
    return jnp.concatenate(a_seg, axis=1)


def _decode_values(vpages, a_ref, acc_ref, fresh):
    n_pages, _, page_tokens = vpages.shape
    for h in range(N_HEADS):
        rows = slice(h * HEAD_DIM, (h + 1) * HEAD_DIM)
        acc = jnp.where(fresh, 0.0, acc_ref[rows, :])
        for p in range(n_pages):
            a_row = a_ref[h:h + 1, p * page_tokens:(p + 1) * page_tokens]
            acc = acc + vpages[p, rows, :] * a_row
        acc_ref[rows, :] = acc


def _attn_kernel(pt_ref, b_ref, q_ref, k_ref, v_ref, tri_ref, ya_ref, sga_ref,
                 x_ref, mod_ref, gattn_ref, wout_ref, gpost_ref,
                 qcol_hbm, bcol_ref, ck_hbm, cv_hbm,
                 y_ref, os_ref,
                 qm_scr, z_scr, a_scr, acc_scr, carry_scr, o_scr,
                 kbuf, vbuf, qbuf, ksem, vsem, qsem, da_scr, dacc_scr, dcarry_scr,
                 chunk_ref, *,
                 extra_chunks):
    blk = ATT_BLOCK
    halves = q_ref.shape[2] // blk
    chains = range(HEADS_PER_GROUP * halves)
    i = pl.program_id(1)
    lane = lax.broadcasted_iota(jnp.int32, (blk, LANES), 1)
    row = lax.broadcasted_iota(jnp.int32, (blk, blk), 0)
    col = lax.broadcasted_iota(jnp.int32, (blk, blk), 1)
    causal = col < row
    tri2 = tri_ref[...]

    n_seq, n_pages = pt_ref.shape
    groups = n_pages // PAGES_PER_CHUNK
    n_chunks = n_seq * groups

    def chunk_copies(c):
        kslot = c % kbuf.shape[0]
        vslot = c % vbuf.shape[0]
        seq = c // groups
        first = (groups - 1 - c % groups) * PAGES_PER_CHUNK
        out = []
        for p in range(PAGES_PER_CHUNK):
            page = pt_ref[seq, first + p]
            out.append(pltpu.make_async_copy(ck_hbm.at[page], kbuf.at[kslot, p],
                                             ksem.at[kslot]))
            out.append(pltpu.make_async_copy(cv_hbm.at[page], vbuf.at[vslot, p],
                                             vsem.at[vslot]))
        return out

    def query_copy(c):
        seq = c // groups
        return pltpu.make_async_copy(qcol_hbm.at[seq], qbuf.at[seq % 2],
                                     qsem.at[seq % 2])

    def start_chunk(c):
        for cp in chunk_copies(c):
            cp.start()

        @pl.when(c % groups == 0)
        def _():
            query_copy(c).start()

    def wait_chunk(c):
        for cp in chunk_copies(c):
            cp.wait()

        @pl.when(c % groups == 0)
        def _():
            query_copy(c).wait()

    def decode_begin():
        c = chunk_ref[0]

        @pl.when(c + CHUNKS_AHEAD < n_chunks)
        def _():
            start_chunk(c + CHUNKS_AHEAD)

        wait_chunk(c)
        return c

    def decode_values(c):
        _decode_values(vbuf.at[(c + vbuf.shape[0] - 1) % vbuf.shape[0]], da_scr,
                       dacc_scr, (c + groups - 1) % groups == 0)

    def finish_sequence(c):
        @pl.when((c % groups == 0) & (c > 0))
        def _():
            os_ref[pl.ds((c - 1) // groups, 1), :] = jnp.sum(dacc_scr[...].T, axis=0,
                                                             keepdims=True)

    def decode_keys(c):
        da_scr[...] = _decode_keys(kbuf.at[c % kbuf.shape[0]], qbuf.at[(c // groups) % 2],
                                   bcol_ref[...], tri2, dcarry_scr, c % groups == 0)

    def decode_compute(c):
        decode_values(c)
        decode_keys(c)

    def decode_end(c):
        finish_sequence(c)
        chunk_ref[0] = c + 1

    def decode_chunk():
        c = decode_begin()
        decode_compute(c)
        decode_end(c)

    first_step = (pl.program_id(0) == 0) & (i == 0)
    last_step = ((pl.program_id(0) == pl.num_programs(0) - 1)
                 & (i == pl.num_programs(1) - 1))

    @pl.when(first_step)
    def _():
        chunk_ref[0] = 0
        dacc_scr[...] = jnp.zeros_like(dacc_scr)
        dcarry_scr[...] = jnp.zeros_like(dcarry_scr)
        da_scr[...] = jnp.zeros_like(da_scr)
        vbuf[vbuf.shape[0] - 1] = jnp.zeros(vbuf.shape[1:], F32)
        for c in range(CHUNKS_AHEAD):
            start_chunk(jnp.int32(c))

    def raw_scores(p, g, key_blk, active):
        off = pl.multiple_of(key_blk * blk, blk)
        kb = k_ref[0, g, pl.ds(off, blk), :]
        qs = jnp.concatenate([qm_scr[p * len(chains) + c] for c in active], axis=0)
        return lax.dot_general(qs, kb, (((1,), (1,)), ((), ())),
                               preferred_element_type=F32)

    def weights(p, g, s, active, masked):
        base = p * len(chains)
        z, sp = [], []
        for n, c in enumerate(active):
            bias2 = b_ref[g * HEADS_PER_GROUP + c // halves] * LOG2E
            zc = s[n * blk:(n + 1) * blk] + bias2
            spc = _softplus2(zc)
            if c in masked:
                spc = jnp.where(causal, spc, 0.0)
            z.append(zc)
            sp.append(spc)
        suffix = jnp.dot(jnp.concatenate(sp, axis=0).astype(BF16), tri2[0:blk],
                         preferred_element_type=F32)
        a = []
        for n, c in enumerate(active):
            carry = carry_scr[base + c]
            c2 = jnp.concatenate([carry] * (blk // LANES), axis=1)
            ac = jnp.exp2(z[n] - suffix[n * blk:(n + 1) * blk] - c2)
            if c in masked:
                ac = jnp.where(causal, ac, 0.0)
            a.append(ac.astype(BF16))
            total = jnp.sum(sp[n], axis=1, keepdims=True)
            carry_scr[base + c] = carry + jnp.broadcast_to(total, (blk, LANES))
        return jnp.concatenate(a, axis=0)

    def accumulate(p, g, key_blk, a, active):
        base = p * len(chains)
        off = pl.multiple_of(key_blk * blk, blk)
        vb = v_ref[0, g, pl.ds(off, blk), :]
        o = jnp.dot(a, vb, preferred_element_type=F32)
        for n, c in enumerate(active):
            acc_scr[base + c] += o[n * blk:(n + 1) * blk]

    def group_body(gg, _):
        groups_here = [(p, gg * GROUPS_PER_BODY + p) for p in range(GROUPS_PER_BODY)]
        for p, g in groups_here:
            qg = q_ref[0, g]
            for c in chains:
                hh, half = divmod(c, halves)
                in_head = (lane >= hh * HEAD_DIM) & (lane < (hh + 1) * HEAD_DIM)
                qh = qg[half * blk:(half + 1) * blk]
                qm_scr[p * len(chains) + c] = jnp.where(in_head, qh, jnp.zeros_like(qh))
        acc_scr[...] = jnp.zeros_like(acc_scr)
        carry_scr[...] = jnp.zeros_like(carry_scr)

        top = halves * i
        everyone = list(chains)
        for p, g in groups_here:
            z_scr[p, 0] = raw_scores(p, g, jnp.maximum(top - 1, 0), everyone)

        for d in reversed(range(1, halves)):
            active = [c for c in chains if c % halves >= d]
            masked = {c for c in active if c % halves == d}
            for p, g in groups_here:
                a = weights(p, g, raw_scores(p, g, top + d, active), active, masked)
                accumulate(p, g, top + d, a, active)
        for p, g in groups_here:
            a_scr[p, 1] = weights(p, g, raw_scores(p, g, top, everyone), everyone,
                                  {c for c in chains if c % halves == 0})

        @pl.when(gg < extra_chunks)
        def _():
            decode_chunk()

        assert halves == 2

        def key_body(m, _, p, g):
            nxt = top - 1 - halves * m
            chunk = decode_begin()
            decode_parts = [decode_values, decode_keys]
            for cur in range(halves):
                accumulate(p, g, nxt - cur + 1, a_scr[p, 1 - cur], everyone)
                z_scr[p, 1 - cur] = raw_scores(p, g, jnp.maximum(nxt - cur - 1, 0),
                                               everyone)
                a_scr[p, cur] = weights(p, g, z_scr[p, cur], everyone, set())
                decode_parts[cur](chunk)
            decode_end(chunk)
            return 0

        for p, g in groups_here:
            lax.fori_loop(0, i, functools.partial(key_body, p=p, g=g), 0)
        for p, g in groups_here:
            accumulate(p, g, 0, a_scr[p, 1], everyone)
            base = p * len(chains)
            heads = [jnp.concatenate([acc_scr[base + hh * halves + half]
                                      for half in range(halves)], axis=0)
                     for hh in range(HEADS_PER_GROUP)]
            lane_q = lax.broadcasted_iota(jnp.int32, heads[0].shape, 1)
            o_scr[g] = jnp.where(lane_q < HEAD_DIM, heads[0], heads[1])
        return 0

    lax.fori_loop(0, N_GROUPS // GROUPS_PER_BODY, group_body, 0)

    o = jnp.concatenate([o_scr[g] for g in range(N_GROUPS)], axis=1)
    y_ref[0] = _mixer_out(o, sga_ref[0], ya_ref[0], x_ref[0],
                          _split_mod(mod_ref[0])[2], gattn_ref[...], wout_ref,
                          gpost_ref[...])

    @pl.when(last_step)
    def _():
        def drain(_, carry):
            decode_chunk()
            return carry

        lax.fori_loop(chunk_ref[0], n_chunks, drain, 0)
        decode_values(n_chunks)
        finish_sequence(n_chunks)


def _attention(page_table, b_sb, q4, k4, v4, tri2, ya, sga, x, mod, g_attn, w_out_b,
               g_post, qcol, bias_col, cache_k, cache_v):
    B, S, D = x.shape
    n_seq, n_pages = page_table.shape
    page_tokens = cache_k.shape[2]
    assert n_pages % PAGES_PER_CHUNK == 0 and KEY_SEGMENT % page_tokens == 0
    blk = ATT_QUERY_TILE
    n_chain = HEADS_PER_GROUP * (blk // ATT_BLOCK)
    cdim = ya.shape[-1]
    n_tiles = S // blk
    n_chunks = n_seq * (n_pages // PAGES_PER_CHUNK)
    bodies = N_GROUPS // GROUPS_PER_BODY
    trips = B * N_GROUPS * sum(range(n_tiles))
    assert trips <= n_chunks, "more key-loop trips than decode chunks"
    extra_chunks = min(bodies, (n_chunks - trips) // (B * n_tiles))
    row_blk = lambda w: pl.BlockSpec((1, blk, w), lambda b, i, pt: (b, i, 0))
    once = pl.Buffered(1)
    const = lambda shp: pl.BlockSpec(shp, lambda b, i, pt: (0,) * len(shp),
                                     pipeline_mode=once)
    whole_seq = pl.BlockSpec((1, N_GROUPS, S, LANES), lambda b, i, pt: (b, 0, 0, 0),
                             pipeline_mode=once)
    page_buf = lambda slots: pltpu.VMEM(
        (slots, PAGES_PER_CHUNK, ATT_DIM, page_tokens), F32)
    grid_spec = pltpu.PrefetchScalarGridSpec(
        num_scalar_prefetch=1,
        grid=(B, S // blk),
        in_specs=[pl.BlockSpec(memory_space=pltpu.SMEM),
                  pl.BlockSpec((1, N_GROUPS, blk, LANES), lambda b, i, pt: (b, 0, i, 0)),
                  whole_seq, whole_seq, const(tri2.shape),
                  row_blk(cdim), row_blk(ATT_DIM), row_blk(D),
                  pl.BlockSpec((1, 1, 3 * D), lambda b, i, pt: (b, 0, 0)),
                  const((1, ATT_DIM)), const(w_out_b.shape), const((1, D)),
                  pl.BlockSpec(memory_space=pl.ANY),
                  const(bias_col.shape),
                  pl.BlockSpec(memory_space=pl.ANY),
                  pl.BlockSpec(memory_space=pl.ANY)],
        out_specs=[row_blk(D),
                   pl.BlockSpec((n_seq, ATT_DIM), lambda b, i, pt: (0, 0))],
        scratch_shapes=[pltpu.VMEM((GROUPS_PER_BODY * n_chain, ATT_BLOCK, LANES), BF16),
                        pltpu.VMEM((GROUPS_PER_BODY, 2, n_chain * ATT_BLOCK, ATT_BLOCK), F32),
                        pltpu.VMEM((GROUPS_PER_BODY, 2, n_chain * ATT_BLOCK, ATT_BLOCK), BF16),
                        pltpu.VMEM((GROUPS_PER_BODY * n_chain, ATT_BLOCK, LANES), F32),
                        pltpu.VMEM((GROUPS_PER_BODY * n_chain, ATT_BLOCK, LANES), F32),
                        pltpu.VMEM((N_GROUPS, blk, LANES), F32),
                        page_buf(CHUNKS_AHEAD + 1), page_buf(CHUNKS_AHEAD + 2),
                        pltpu.VMEM((2, ATT_DIM, LANES), F32),
                        pltpu.SemaphoreType.DMA((CHUNKS_AHEAD + 1,)),
                        pltpu.SemaphoreType.DMA((CHUNKS_AHEAD + 2,)),
                        pltpu.SemaphoreType.DMA((2,)),
                        pltpu.VMEM((N_HEADS, PAGES_PER_CHUNK * page_tokens), F32),
                        pltpu.VMEM((ATT_DIM, LANES), F32),
                        pltpu.VMEM((N_HEADS, LANES), F32),
                        pltpu.SMEM((1,), jnp.int32)],
    )
    return pl.pallas_call(
        functools.partial(_attn_kernel, extra_chunks=extra_chunks),
        grid_spec=grid_spec,
        out_shape=[jax.ShapeDtypeStruct((B, S, D), F32),
                   jax.ShapeDtypeStruct((n_seq, ATT_DIM), F32)],
        compiler_params=pltpu.CompilerParams(
            dimension_semantics=("arbitrary", "arbitrary"),
            vmem_limit_bytes=VMEM_LIMIT),
        name="attention",
    )(page_table, b_sb, q4, k4, v4, tri2, ya, sga, x, mod, g_attn, w_out_b, g_post,
      qcol, bias_col, cache_k, cache_v)


def _sample_proj_kernel(x_ref, mod_ref, st_ref, gpre_ref, w_ref, cw_ref,
                        gconv_ref, ya_ref, q_ref, k_ref, v_ref, ga_ref, conv_ref):
    cdim = cw_ref.shape[1]
    h = _modulated_norm(x_ref[...], mod_ref[...], gpre_ref[...])
    p = jnp.dot(h, w_ref[...], preferred_element_type=F32)
    seg = lambda i: p[:, i * cdim:(i + 1) * cdim]
    u = seg(2) * seg(0)
    u2, u1 = st_ref[:, 0:cdim], st_ref[:, cdim:2 * cdim]
    ya_ref[...] = _conv_branch(u, u1, u2, seg(1), seg(3), cw_ref[...],
                               gconv_ref[...]).astype(BF16)
    conv_ref[:, 0:cdim] = u1
    conv_ref[:, cdim:2 * cdim] = u
    q_ref[...] = seg(4) * (HEAD_DIM ** -0.5)
    k_ref[...] = seg(5)
    v_ref[...] = seg(6)
    ga = seg(7)
    ga_ref[...] = ga * jax.nn.sigmoid(ga)


def _sample_proj(x, mod, state, g_pre, w_in_b, conv_w, g_conv):
    n, _ = x.shape
    cdim = conv_w.shape[1]
    sds = lambda w, dt=F32: jax.ShapeDtypeStruct((n, w), dt)
    return pl.pallas_call(
        _sample_proj_kernel,
        out_shape=[sds(cdim, BF16), sds(ATT_DIM), sds(ATT_DIM), sds(ATT_DIM),
                   sds(ATT_DIM), sds((CONV_WIDTH - 1) * cdim)],
        compiler_params=pltpu.CompilerParams(vmem_limit_bytes=VMEM_LIMIT),
        name="sample_proj",
    )(x, mod, state, g_pre, w_in_b, conv_w, g_conv)


def _sample_out_kernel(o_ref, sga_ref, ya_ref, x_ref, mod_ref, gattn_ref,
                       wout_ref, gpost_ref, y_ref):
    y_ref[...] = _mixer_out(o_ref[...], sga_ref[...], ya_ref[...], x_ref[...],
                            _split_mod(mod_ref[...])[2], gattn_ref[...], wout_ref,
                            gpost_ref[...])


def _sample_out(o, sga, ya, x, mod, g_attn, w_out_b, g_post):
    return pl.pallas_call(
        _sample_out_kernel,
        out_shape=jax.ShapeDtypeStruct(x.shape, F32),
        compiler_params=pltpu.CompilerParams(vmem_limit_bytes=VMEM_LIMIT),
        name="sample_out",
    )(o, sga, ya, x, mod, g_attn, w_out_b, g_post)


def _layer(xp, xs, c_all, pool_k, pool_v, state, page_table, w_ada, b_ada, g_pre,
           w_in, conv_w, g_conv, b_sb, g_attn, w_out, g_post):
    B, S, D = xp.shape
    n_seq = xs.shape[0]
    row = lambda v: v.reshape(1, -1)
    w_in_b = w_in.astype(BF16)
    w_out_b = w_out.astype(BF16)

    mod = _ada(c_all, w_ada, row(b_ada))
    mod_p, mod_s = mod[:B].reshape(B, 1, 3 * D), mod[B:B + n_seq]

    ya, q4, k, kb4, v, vb4, sga, conv_p = _prompt_proj(
        xp, mod_p, row(g_pre), w_in_b, conv_w, row(g_conv))
    xs2 = xs.reshape(n_seq, D)
    ya_s, q_s, k_s, v_s, sga_s, conv_s = _sample_proj(
        xs2, mod_s, state.reshape(n_seq, -1), row(g_pre), w_in_b, conv_w,
        row(g_conv))
    conv_s = conv_s.reshape(state.shape)

    idx = jnp.arange(ATT_BLOCK)
    tri = (idx[:, None] >= idx[None, :]).astype(BF16)
    tri2 = jnp.concatenate([tri, tri], axis=0)
    n_pool, page_tokens = pool_k.shape[:2]
    assert page_tokens == LANES
    pages = lambda pool: jnp.transpose(pool, (0, 2, 3, 1)).reshape(
        n_pool, ATT_DIM, page_tokens)
    qcol = jnp.broadcast_to(q_s[:, :, None], (n_seq, ATT_DIM, LANES))
    bias_col = jnp.broadcast_to(b_sb[:, None], (N_HEADS, LANES))
    yp, o_s = _attention(page_table, b_sb, q4, kb4, vb4, tri2, ya, sga, xp, mod_p,
                         row(g_attn), w_out_b, row(g_post), qcol, bias_col,
                         pages(pool_k), pages(pool_v))
    ys = _sample_out(o_s, sga_s, ya_s, xs2, mod_s, row(g_attn), w_out_b, row(g_post))

    heads = lambda t, n, L: t.reshape(n, L, N_HEADS, HEAD_DIM)
    return (yp, ys.reshape(xs.shape), heads(k, B, S), heads(v, B, S), conv_p,
            heads(k_s, n_seq, 1), heads(v_s, n_seq, 1), conv_s)


def kernel(x_prompt, x_sample, c_prompt, c_sample, cache_k, cache_v, state_conv,
           page_table, w_ada, b_ada, g_pre, w_in, conv_w, g_conv, b_sb, g_attn,
           w_out, g_post):
    assert x_sample.shape[1] == 1, "decode step handles one new token per sequence"
    depth = w_in.shape[0]
    n_rows = c_prompt.shape[0] + c_sample.shape[0]
    pad = (-n_rows) % 8
    c_all = jnp.concatenate(
        [c_prompt, c_sample, jnp.zeros((pad, c_prompt.shape[1]), F32)], axis=0)
    xp, xs = x_prompt, x_sample
    outs = [[] for _ in range(6)]
    for l in range(depth):
        xp, xs, kp, vp, cp, ksn, vsn, csn = _layer(
            xp, xs, c_all, cache_k[l], cache_v[l], state_conv[l], page_table,
            w_ada[l], b_ada[l], g_pre[l], w_in[l], conv_w[l], g_conv[l], b_sb[l],
            g_attn[l], w_out[l], g_post[l])
        for lst, t in zip(outs, (kp, vp, cp, ksn, vsn, csn)):
            lst.append(t)
    return (xp, xs) + tuple(jnp.stack(t) for t in outs)
```

```python
import functools

import jax
import jax.numpy as jnp
from jax import lax
from jax.experimental import pallas as pl
from jax.experimental.pallas import tpu as pltpu

N_HEADS = 8
HEAD_DIM = 64
ATT_DIM = N_HEADS * HEAD_DIM
CONV_WIDTH = 3
EPS = 1e-6

LANES = 128
HEADS_PER_GROUP = LANES // HEAD_DIM
N_GROUPS = ATT_DIM // LANES

PROJ_ROWS = 512
ATT_BLOCK = 256
ATT_QUERY_TILE = 2 * ATT_BLOCK
GROUPS_PER_BODY = 2
PAGES_PER_CHUNK = 8
CHUNKS_AHEAD = 2
KEY_SEGMENT = ATT_BLOCK
VMEM_LIMIT = 56 * 1024 * 1024

F32 = jnp.float32
BF16 = jnp.bfloat16
LOG2E = 1.4426950408889634
SOFTPLUS2_LINEAR_ABOVE = 64.0


def _rms(x, g):
    ms = jnp.mean(x * x, axis=-1, keepdims=True)
    return x * lax.rsqrt(ms + EPS) * g


def _softplus(z):
    return jnp.maximum(z, 0.0) + jnp.log(1.0 + jnp.exp(-jnp.abs(z)))


def _softplus2(z2):
    return jnp.where(z2 > SOFTPLUS2_LINEAR_ABOVE, z2,
                     jnp.log2(1.0 + jnp.exp2(z2)))


def _split_bf16(x):
    hi = x.astype(BF16)
    lo = (x - hi.astype(F32)).astype(BF16)
    return hi, lo


def _suffix_sum(sp, tri2):
    hi, lo = _split_bf16(sp)
    return jnp.dot(jnp.concatenate([hi, lo], axis=1), tri2,
                   preferred_element_type=F32)


def _ada_kernel(c_ref, w_ref, b_ref, o_ref):
    c = c_ref[...]
    a = c * jax.nn.sigmoid(c)
    o_ref[...] = jnp.dot(a, w_ref[...], preferred_element_type=F32,
                         precision=lax.Precision.HIGHEST) + b_ref[...]


def _ada(c_all, w_ada, b_ada):
    rows = c_all.shape[0]
    return pl.pallas_call(
        _ada_kernel,
        out_shape=jax.ShapeDtypeStruct((rows, w_ada.shape[1]), F32),
        compiler_params=pltpu.CompilerParams(vmem_limit_bytes=VMEM_LIMIT),
        name="ada",
    )(c_all, w_ada, b_ada)


def _split_mod(mod):
    d = mod.shape[-1] // 3
    return mod[:, 0:d], mod[:, d:2 * d], mod[:, 2 * d:3 * d]


def _modulated_norm(x, mod, g_pre):
    shift, scale, _ = _split_mod(mod)
    return (_rms(x, g_pre) * (1.0 + scale) + shift).astype(BF16)


def _conv_branch(u, u1, u2, bc, gc, conv_w, g_conv):
    conv_y = conv_w[0:1] * u2 + conv_w[1:2] * u1 + conv_w[2:3] * u
    return _rms(bc * conv_y, g_conv) * (gc * jax.nn.sigmoid(gc))


def _prompt_proj_kernel(x_ref, mod_ref, gpre_ref, w_ref, cw_ref, gconv_ref,
                        ya_ref, q_ref, k_ref, kb_ref, v_ref, vb_ref, ga_ref,
                        conv_ref, u_scr):
    rows = x_ref.shape[1]
    cdim = cw_ref.shape[1]
    h = _modulated_norm(x_ref[0], mod_ref[0], gpre_ref[...])

    def seg(i):
        return jnp.dot(h, w_ref[:, i * cdim:(i + 1) * cdim],
                       preferred_element_type=F32)

    @pl.when(pl.program_id(1) == 0)
    def _():
        u_scr[0:8, :] = jnp.zeros((8, cdim), F32)

    u = seg(2) * seg(0)
    u_scr[8:8 + rows, :] = u
    u1 = u_scr[7:7 + rows, :]
    u2 = u_scr[6:6 + rows, :]
    ya = _conv_branch(u, u1, u2, seg(1), seg(3), cw_ref[...], gconv_ref[...])
    ya_ref[0] = ya.astype(BF16)
    tail = u_scr[rows + 6:rows + 8, :]
    conv_ref[0] = tail
    u_scr[6:8, :] = tail

    q = (seg(4) * (LOG2E * HEAD_DIM ** -0.5)).astype(BF16)
    k = seg(5)
    v = seg(6)
    k_ref[0] = k
    v_ref[0] = v
    kb = k.astype(BF16)
    vb = v.astype(BF16)
    for g in range(N_GROUPS):
        sl = slice(g * LANES, (g + 1) * LANES)
        q_ref[0, g] = q[:, sl]
        kb_ref[0, g] = kb[:, sl]
        vb_ref[0, g] = vb[:, sl]
    ga = seg(7)
    ga_ref[0] = ga * jax.nn.sigmoid(ga)


def _prompt_proj(x, mod, g_pre, w_in_b, conv_w, g_conv):
    B, S, D = x.shape
    cdim = conv_w.shape[1]
    rows = PROJ_ROWS
    grid = (B, S // rows)
    row_blk = lambda w: pl.BlockSpec((1, rows, w), lambda b, s: (b, s, 0))
    grp_blk = pl.BlockSpec((1, N_GROUPS, rows, LANES), lambda b, s: (b, 0, s, 0))
    const = lambda shp: pl.BlockSpec(shp, lambda b, s: (0,) * len(shp))
    grp_shape = jax.ShapeDtypeStruct((B, N_GROUPS, S, LANES), BF16)
    return pl.pallas_call(
        _prompt_proj_kernel,
        grid=grid,
        in_specs=[row_blk(D),
                  pl.BlockSpec((1, 1, 3 * D), lambda b, s: (b, 0, 0)),
                  const((1, D)), const(w_in_b.shape), const(conv_w.shape),
                  const((1, cdim))],
        out_specs=[row_blk(cdim), grp_blk, row_blk(ATT_DIM), grp_blk,
                   row_blk(ATT_DIM), grp_blk, row_blk(ATT_DIM),
                   pl.BlockSpec((1, CONV_WIDTH - 1, cdim), lambda b, s: (b, 0, 0))],
        out_shape=[jax.ShapeDtypeStruct((B, S, cdim), BF16), grp_shape,
                   jax.ShapeDtypeStruct((B, S, ATT_DIM), F32), grp_shape,
                   jax.ShapeDtypeStruct((B, S, ATT_DIM), F32), grp_shape,
                   jax.ShapeDtypeStruct((B, S, ATT_DIM), F32),
                   jax.ShapeDtypeStruct((B, CONV_WIDTH - 1, cdim), F32)],
        scratch_shapes=[pltpu.VMEM((rows + 8, cdim), F32)],
        compiler_params=pltpu.CompilerParams(
            dimension_semantics=("arbitrary", "arbitrary"),
            vmem_limit_bytes=VMEM_LIMIT),
        name="prompt_proj",
    )(x, mod, g_pre, w_in_b, conv_w, g_conv)


def _mixer_out(o, sga, ya, x, gate, g_attn, w_out_ref, g_post):
    cdim = ya.shape[-1]
    yb = (_rms(o, g_attn) * sga).astype(BF16)
    m = (jnp.dot(ya, w_out_ref[0:cdim, :], preferred_element_type=F32)
         + jnp.dot(yb, w_out_ref[cdim:, :], preferred_element_type=F32))
    return x + gate * _rms(m, g_post)


def _decode_keys(kpages, qcol_ref, bias, tri2, carry_ref, fresh):
    n_pages, _, page_tokens = kpages.shape
    seg_pages = KEY_SEGMENT // page_tokens
    n_seg = n_pages // seg_pages
    sub = 8

    z_rows = [[None] * N_HEADS for _ in range(n_pages)]
    for h in range(N_HEADS):
        part = [None] * n_pages
        for r in range(HEAD_DIM // sub):
            rows = slice(h * HEAD_DIM + r * sub, h * HEAD_DIM + (r + 1) * sub)
            q = qcol_ref[rows, :]
            for p in range(n_pages):
                t = kpages[p, rows, :] * q
                part[p] = t if r == 0 else part[p] + t
        for p in range(n_pages):
            z_rows[p][h] = jnp.sum(part[p], axis=0, keepdims=True)
    z_seg, sp_seg = [], []
    for s in range(n_seg):
        z = jnp.concatenate(
            [jnp.concatenate(z_rows[p], axis=0)
             for p in range(s * seg_pages, (s + 1) * seg_pages)], axis=1)
        z = z + jnp.concatenate([bias] * seg_pages, axis=1)
        z_seg.append(z)
        sp_seg.append(_softplus(z))

    suffix = _suffix_sum(jnp.concatenate(sp_seg, axis=0), tri2)

    carry = jnp.where(fresh, 0.0, carry_ref[...])
    a_seg = [None] * n_seg
    for s in reversed(range(n_seg)):
        c = jnp.concatenate([carry] * seg_pages, axis=1)
        a_seg[s] = jnp.exp(z_seg[s] - suffix[s * N_HEADS:(s + 1) * N_HEADS] - c)
        total = jnp.sum(sp_seg[s], axis=1, keepdims=True)
        carry = carry + jnp.broadcast_to(total, carry.shape)
    carry_ref[...] = carry
    return jnp.concatenate(a_seg, axis=1)


def _decode_values(vpages, a_ref, acc_ref, fresh):
    n_pages, _, page_tokens = vpages.shape
    for h in range(N_HEADS):
        rows = slice(h * HEAD_DIM, (h + 1) * HEAD_DIM)
        acc = jnp.where(fresh, 0.0, acc_ref[rows, :])
        for p in range(n_pages):
            a_row = a_ref[h:h + 1, p * page_tokens:(p + 1) * page_tokens]
            acc = acc + vpages[p, rows, :] * a_row
        acc_ref[rows, :] = acc


def _attn_kernel(pt_ref, b_ref, q_ref, k_ref, v_ref, tri_ref, ya_ref, sga_ref,
                 x_ref, mod_ref, gattn_ref, wout_ref, gpost_ref,
                 qcol_hbm, bcol_ref, ck_hbm, cv_hbm,
                 y_ref, os_ref,
                 qm_scr, z_scr, a_scr, acc_scr, carry_scr, o_scr,
                 kbuf, vbuf, qbuf, ksem, vsem, qsem, da_scr, dacc_scr, dcarry_scr,
                 chunk_ref, *,
                 extra_chunks):
    blk = ATT_BLOCK
    halves = q_ref.shape[2] // blk
    chains = range(HEADS_PER_GROUP * halves)
    i = pl.program_id(1)
    lane = lax.broadcasted_iota(jnp.int32, (blk, LANES), 1)
    row = lax.broadcasted_iota(jnp.int32, (blk, blk), 0)
    col = lax.broadcasted_iota(jnp.int32, (blk, blk), 1)
    causal = col < row
    tri2 = tri_ref[...]

    n_seq, n_pages = pt_ref.shape
    groups = n_pages // PAGES_PER_CHUNK
    n_chunks = n_seq * groups

    def chunk_copies(c):
        kslot = c % kbuf.shape[0]
        vslot = c % vbuf.shape[0]
        seq = c // groups
        first = (groups - 1 - c % groups) * PAGES_PER_CHUNK
        out = []
        for p in range(PAGES_PER_CHUNK):
            page = pt_ref[seq, first + p]
            out.append(pltpu.make_async_copy(ck_hbm.at[page], kbuf.at[kslot, p],
                                             ksem.at[kslot]))
            out.append(pltpu.make_async_copy(cv_hbm.at[page], vbuf.at[vslot, p],
                                             vsem.at[vslot]))
        return out

    def query_copy(c):
        seq = c // groups
        return pltpu.make_async_copy(qcol_hbm.at[seq], qbuf.at[seq % 2],
                                     qsem.at[seq % 2])

    def start_chunk(c):
        for n, cp in enumerate(chunk_copies(c)):
            cp.start(priority=n % 2)

        @pl.when(c % groups == 0)
        def _():
            query_copy(c).start()

    def wait_chunk(c):
        for cp in chunk_copies(c):
            cp.wait()

        @pl.when(c % groups == 0)
        def _():
            query_copy(c).wait()

    def decode_begin():
        c = chunk_ref[0]

        @pl.when(c + CHUNKS_AHEAD < n_chunks)
        def _():
            start_chunk(c + CHUNKS_AHEAD)

        wait_chunk(c)
        return c

    def decode_values(c):
        _decode_values(vbuf.at[(c + vbuf.shape[0] - 1) % vbuf.shape[0]], da_scr,
                       dacc_scr, (c + groups - 1) % groups == 0)

    def finish_sequence(c):
        @pl.when((c % groups == 0) & (c > 0))
        def _():
            os_ref[pl.ds((c - 1) // groups, 1), :] = jnp.sum(dacc_scr[...].T, axis=0,
                                                             keepdims=True)

    def decode_keys(c):
        da_scr[...] = _decode_keys(kbuf.at[c % kbuf.shape[0]], qbuf.at[(c // groups) % 2],
                                   bcol_ref[...], tri2, dcarry_scr, c % groups == 0)

    def decode_compute(c):
        decode_values(c)
        decode_keys(c)

    def decode_end(c):
        finish_sequence(c)
        chunk_ref[0] = c + 1

    def decode_chunk():
        c = decode_begin()
        decode_compute(c)
        decode_end(c)

    first_step = (pl.program_id(0) == 0) & (i == 0)
    last_step = ((pl.program_id(0) == pl.num_programs(0) - 1)
                 & (i == pl.num_programs(1) - 1))

    @pl.when(first_step)
    def _():
        chunk_ref[0] = 0
        dacc_scr[...] = jnp.zeros_like(dacc_scr)
        dcarry_scr[...] = jnp.zeros_like(dcarry_scr)
        da_scr[...] = jnp.zeros_like(da_scr)
        vbuf[vbuf.shape[0] - 1] = jnp.zeros(vbuf.shape[1:], F32)
        for c in range(CHUNKS_AHEAD):
            start_chunk(jnp.int32(c))

    def raw_scores(p, g, key_blk, active):
        off = pl.multiple_of(key_blk * blk, blk)
        kb = k_ref[0, g, pl.ds(off, blk), :]
        qs = jnp.concatenate([qm_scr[p * len(chains) + c] for c in active], axis=0)
        return lax.dot_general(qs, kb, (((1,), (1,)), ((), ())),
                               preferred_element_type=F32)

    def weights(p, g, s, active, masked):
        base = p * len(chains)
        z, sp = [], []
        for n, c in enumerate(active):
            bias2 = b_ref[g * HEADS_PER_GROUP + c // halves] * LOG2E
            zc = s[n * blk:(n + 1) * blk] + bias2
            spc = _softplus2(zc)
            if c in masked:
                spc = jnp.where(causal, spc, 0.0)
            z.append(zc)
            sp.append(spc)
        suffix = jnp.dot(jnp.concatenate(sp, axis=0).astype(BF16), tri2[0:blk],
                         preferred_element_type=F32)
        a = []
        for n, c in enumerate(active):
            carry = carry_scr[base + c]
            c2 = jnp.concatenate([carry] * (blk // LANES), axis=1)
            ac = jnp.exp2(z[n] - suffix[n * blk:(n + 1) * blk] - c2)
            if c in masked:
                ac = jnp.where(causal, ac, 0.0)
            a.append(ac.astype(BF16))
            total = jnp.sum(sp[n], axis=1, keepdims=True)
            carry_scr[base + c] = carry + jnp.broadcast_to(total, (blk, LANES))
        return jnp.concatenate(a, axis=0)

    def accumulate(p, g, key_blk, a, active):
        base = p * len(chains)
        off = pl.multiple_of(key_blk * blk, blk)
        vb = v_ref[0, g, pl.ds(off, blk), :]
        o = jnp.dot(a, vb, preferred_element_type=F32)
        for n, c in enumerate(active):
            acc_scr[base + c] += o[n * blk:(n + 1) * blk]

    def group_body(gg, _):
        groups_here = [(p, gg * GROUPS_PER_BODY + p) for p in range(GROUPS_PER_BODY)]
        for p, g in groups_here:
            qg = q_ref[0, g]
            for c in chains:
                hh, half = divmod(c, halves)
                in_head = (lane >= hh * HEAD_DIM) & (lane < (hh + 1) * HEAD_DIM)
                qh = qg[half * blk:(half + 1) * blk]
                qm_scr[p * len(chains) + c] = jnp.where(in_head, qh, jnp.zeros_like(qh))
        acc_scr[...] = jnp.zeros_like(acc_scr)
        carry_scr[...] = jnp.zeros_like(carry_scr)

        top = halves * i
        everyone = list(chains)
        for p, g in groups_here:
            z_scr[p, 0] = raw_scores(p, g, jnp.maximum(top - 1, 0), everyone)

        for d in reversed(range(1, halves)):
            active = [c for c in chains if c % halves >= d]
            masked = {c for c in active if c % halves == d}
            for p, g in groups_here:
                a = weights(p, g, raw_scores(p, g, top + d, active), active, masked)
                accumulate(p, g, top + d, a, active)
        for p, g in groups_here:
            a_scr[p, 1] = weights(p, g, raw_scores(p, g, top, everyone), everyone,
                                  {c for c in chains if c % halves == 0})

        @pl.when(gg < extra_chunks)
        def _():
            decode_chunk()

        assert halves == 2

        def key_body(m, _, p, g):
            nxt = top - 1 - halves * m
            chunk = decode_begin()
            decode_parts = [decode_values, decode_keys]
            for cur in range(halves):
                accumulate(p, g, nxt - cur + 1, a_scr[p, 1 - cur], everyone)
                z_scr[p, 1 - cur] = raw_scores(p, g, jnp.maximum(nxt - cur - 1, 0),
                                               everyone)
                a_scr[p, cur] = weights(p, g, z_scr[p, cur], everyone, set())
                decode_parts[cur](chunk)
            decode_end(chunk)
            return 0

        for p, g in groups_here:
            lax.fori_loop(0, i, functools.partial(key_body, p=p, g=g), 0)
        for p, g in groups_here:
            accumulate(p, g, 0, a_scr[p, 1], everyone)
            base = p * len(chains)
            heads = [jnp.concatenate([acc_scr[base + hh * halves + half]
                                      for half in range(halves)], axis=0)
                     for hh in range(HEADS_PER_GROUP)]
            lane_q = lax.broadcasted_iota(jnp.int32, heads[0].shape, 1)
            o_scr[g] = jnp.where(lane_q < HEAD_DIM, heads[0], heads[1])
        return 0

    lax.fori_loop(0, N_GROUPS // GROUPS_PER_BODY, group_body, 0)

    o = jnp.concatenate([o_scr[g] for g in range(N_GROUPS)], axis=1)
    y_ref[0] = _mixer_out(o, sga_ref[0], ya_ref[0], x_ref[0],
                          _split_mod(mod_ref[0])[2], gattn_ref[...], wout_ref,
                          gpost_ref[...])

    @pl.when(last_step)
    def _():
        def drain(_, carry):
            decode_chunk()
            return carry

        lax.fori_loop(chunk_ref[0], n_chunks, drain, 0)
        decode_values(n_chunks)
        finish_sequence(n_chunks)


def _attention(page_table, b_sb, q4, k4, v4, tri2, ya, sga, x, mod, g_attn, w_out_b,
               g_post, qcol, bias_col, cache_k, cache_v):
    B, S, D = x.shape
    n_seq, n_pages = page_table.shape
    page_tokens = cache_k.shape[2]
    assert n_pages % PAGES_PER_CHUNK == 0 and KEY_SEGMENT % page_tokens == 0
    blk = ATT_QUERY_TILE
    n_chain = HEADS_PER_GROUP * (blk // ATT_BLOCK)
    cdim = ya.shape[-1]
    n_tiles = S // blk
    n_chunks = n_seq * (n_pages // PAGES_PER_CHUNK)
    bodies = N_GROUPS // GROUPS_PER_BODY
    trips = B * N_GROUPS * sum(range(n_tiles))
    assert trips <= n_chunks, "more key-loop trips than decode chunks"
    extra_chunks = min(bodies, (n_chunks - trips) // (B * n_tiles))
    row_blk = lambda w: pl.BlockSpec((1, blk, w), lambda b, i, pt: (b, i, 0))
    once = pl.Buffered(1)
    const = lambda shp: pl.BlockSpec(shp, lambda b, i, pt: (0,) * len(shp),
                                     pipeline_mode=once)
    whole_seq = pl.BlockSpec((1, N_GROUPS, S, LANES), lambda b, i, pt: (b, 0, 0, 0),
                             pipeline_mode=once)
    page_buf = lambda slots: pltpu.VMEM(
        (slots, PAGES_PER_CHUNK, ATT_DIM, page_tokens), F32)
    grid_spec = pltpu.PrefetchScalarGridSpec(
        num_scalar_prefetch=1,
        grid=(B, S // blk),
        in_specs=[pl.BlockSpec(memory_space=pltpu.SMEM),
                  pl.BlockSpec((1, N_GROUPS, blk, LANES), lambda b, i, pt: (b, 0, i, 0)),
                  whole_seq, whole_seq, const(tri2.shape),
                  row_blk(cdim), row_blk(ATT_DIM), row_blk(D),
                  pl.BlockSpec((1, 1, 3 * D), lambda b, i, pt: (b, 0, 0)),
                  const((1, ATT_DIM)), const(w_out_b.shape), const((1, D)),
                  pl.BlockSpec(memory_space=pl.ANY),
                  const(bias_col.shape),
                  pl.BlockSpec(memory_space=pl.ANY),
                  pl.BlockSpec(memory_space=pl.ANY)],
        out_specs=[row_blk(D),
                   pl.BlockSpec((n_seq, ATT_DIM), lambda b, i, pt: (0, 0))],
        scratch_shapes=[pltpu.VMEM((GROUPS_PER_BODY * n_chain, ATT_BLOCK, LANES), BF16),
                        pltpu.VMEM((GROUPS_PER_BODY, 2, n_chain * ATT_BLOCK, ATT_BLOCK), F32),
                        pltpu.VMEM((GROUPS_PER_BODY, 2, n_chain * ATT_BLOCK, ATT_BLOCK), BF16),
                        pltpu.VMEM((GROUPS_PER_BODY * n_chain, ATT_BLOCK, LANES), F32),
                        pltpu.VMEM((GROUPS_PER_BODY * n_chain, ATT_BLOCK, LANES), F32),
                        pltpu.VMEM((N_GROUPS, blk, LANES), F32),
                        page_buf(CHUNKS_AHEAD + 1), page_buf(CHUNKS_AHEAD + 2),
                        pltpu.VMEM((2, ATT_DIM, LANES), F32),
                        pltpu.SemaphoreType.DMA((CHUNKS_AHEAD + 1,)),
                        pltpu.SemaphoreType.DMA((CHUNKS_AHEAD + 2,)),
                        pltpu.SemaphoreType.DMA((2,)),
                        pltpu.VMEM((N_HEADS, PAGES_PER_CHUNK * page_tokens), F32),
                        pltpu.VMEM((ATT_DIM, LANES), F32),
                        pltpu.VMEM((N_HEADS, LANES), F32),
                        pltpu.SMEM((1,), jnp.int32)],
    )
    return pl.pallas_call(
        functools.partial(_attn_kernel, extra_chunks=extra_chunks),
        grid_spec=grid_spec,
        out_shape=[jax.ShapeDtypeStruct((B, S, D), F32),
                   jax.ShapeDtypeStruct((n_seq, ATT_DIM), F32)],
        compiler_params=pltpu.CompilerParams(
            dimension_semantics=("arbitrary", "arbitrary"),
            vmem_limit_bytes=VMEM_LIMIT),
        name="attention",
    )(page_table, b_sb, q4, k4, v4, tri2, ya, sga, x, mod, g_attn, w_out_b, g_post,
      qcol, bias_col, cache_k, cache_v)


def _sample_proj_kernel(x_ref, mod_ref, st_ref, gpre_ref, w_ref, cw_ref,
                        gconv_ref, ya_ref, q_ref, k_ref, v_ref, ga_ref, conv_ref):
    cdim = cw_ref.shape[1]
    h = _modulated_norm(x_ref[...], mod_ref[...], gpre_ref[...])
    p = jnp.dot(h, w_ref[...], preferred_element_type=F32)
    seg = lambda i: p[:, i * cdim:(i + 1) * cdim]
    u = seg(2) * seg(0)
    u2, u1 = st_ref[:, 0:cdim], st_ref[:, cdim:2 * cdim]
    ya_ref[...] = _conv_branch(u, u1, u2, seg(1), seg(3), cw_ref[...],
                               gconv_ref[...]).astype(BF16)
    conv_ref[:, 0:cdim] = u1
    conv_ref[:, cdim:2 * cdim] = u
    q_ref[...] = seg(4) * (HEAD_DIM ** -0.5)
    k_ref[...] = seg(5)
    v_ref[...] = seg(6)
    ga = seg(7)
    ga_ref[...] = ga * jax.nn.sigmoid(ga)


def _sample_proj(x, mod, state, g_pre, w_in_b, conv_w, g_conv):
    n, _ = x.shape
    cdim = conv_w.shape[1]
    sds = lambda w, dt=F32: jax.ShapeDtypeStruct((n, w), dt)
    return pl.pallas_call(
        _sample_proj_kernel,
        out_shape=[sds(cdim, BF16), sds(ATT_DIM), sds(ATT_DIM), sds(ATT_DIM),
                   sds(ATT_DIM), sds((CONV_WIDTH - 1) * cdim)],
        compiler_params=pltpu.CompilerParams(vmem_limit_bytes=VMEM_LIMIT),
        name="sample_proj",
    )(x, mod, state, g_pre, w_in_b, conv_w, g_conv)


def _sample_out_kernel(o_ref, sga_ref, ya_ref, x_ref, mod_ref, gattn_ref,
                       wout_ref, gpost_ref, y_ref):
    y_ref[...] = _mixer_out(o_ref[...], sga_ref[...], ya_ref[...], x_ref[...],
                            _split_mod(mod_ref[...])[2], gattn_ref[...], wout_ref,
                            gpost_ref[...])


def _sample_out(o, sga, ya, x, mod, g_attn, w_out_b, g_post):
    return pl.pallas_call(
        _sample_out_kernel,
        out_shape=jax.ShapeDtypeStruct(x.shape, F32),
        compiler_params=pltpu.CompilerParams(vmem_limit_bytes=VMEM_LIMIT),
        name="sample_out",
    )(o, sga, ya, x, mod, g_attn, w_out_b, g_post)


def _layer(xp, xs, c_all, pool_k, pool_v, state, page_table, w_ada, b_ada, g_pre,
           w_in, conv_w, g_conv, b_sb, g_attn, w_out, g_post):
    B, S, D = xp.shape
    n_seq = xs.shape[0]
    row = lambda v: v.reshape(1, -1)
    w_in_b = w_in.astype(BF16)
    w_out_b = w_out.astype(BF16)

    mod = _ada(c_all, w_ada, row(b_ada))
    mod_p, mod_s = mod[:B].reshape(B, 1, 3 * D), mod[B:B + n_seq]

    ya, q4, k, kb4, v, vb4, sga, conv_p = _prompt_proj(
        xp, mod_p, row(g_pre), w_in_b, conv_w, row(g_conv))
    xs2 = xs.reshape(n_seq, D)
    ya_s, q_s, k_s, v_s, sga_s, conv_s = _sample_proj(
        xs2, mod_s, state.reshape(n_seq, -1), row(g_pre), w_in_b, conv_w,
        row(g_conv))
    conv_s = conv_s.reshape(state.shape)

    idx = jnp.arange(ATT_BLOCK)
    tri = (idx[:, None] >= idx[None, :]).astype(BF16)
    tri2 = jnp.concatenate([tri, tri], axis=0)
    n_pool, page_tokens = pool_k.shape[:2]
    assert page_tokens == LANES
    pages = lambda pool: jnp.transpose(pool, (0, 2, 3, 1)).reshape(
        n_pool, ATT_DIM, page_tokens)
    qcol = jnp.broadcast_to(q_s[:, :, None], (n_seq, ATT_DIM, LANES))
    bias_col = jnp.broadcast_to(b_sb[:, None], (N_HEADS, LANES))
    yp, o_s = _attention(page_table, b_sb, q4, kb4, vb4, tri2, ya, sga, xp, mod_p,
                         row(g_attn), w_out_b, row(g_post), qcol, bias_col,
                         pages(pool_k), pages(pool_v))
    ys = _sample_out(o_s, sga_s, ya_s, xs2, mod_s, row(g_attn), w_out_b, row(g_post))

    heads = lambda t, n, L: t.reshape(n, L, N_HEADS, HEAD_DIM)
    return (yp, ys.reshape(xs.shape), heads(k, B, S), heads(v, B, S), conv_p,
            heads(k_s, n_seq, 1), heads(v_s, n_seq, 1), conv_s)


def kernel(x_prompt, x_sample, c_prompt, c_sample, cache_k, cache_v, state_conv,
           page_table, w_ada, b_ada, g_pre, w_in, conv_w, g_conv, b_sb, g_attn,
           w_out, g_post):
    assert x_sample.shape[1] == 1, "decode step handles one new token per sequence"
    depth = w_in.shape[0]
    n_rows = c_prompt.shape[0] + c_sample.shape[0]
    pad = (-n_rows) % 8
    c_all = jnp.concatenate(
        [c_prompt, c_sample, jnp.zeros((pad, c_prompt.shape[1]), F32)], axis=0)
    xp, xs = x_prompt, x_sample
    outs = [[] for _ in range(6)]
    for l in range(depth):
        xp, xs, kp, vp, cp, ksn, vsn, csn = _layer(
            xp, xs, c_all, cache_k[l], cache_v[l], state_conv[l], page_table,
            w_ada[l], b_ada[l], g_pre[l], w_in[l], conv_w[l], g_conv[l], b_sb[l],
            g_attn[l], w_out[l], g_post[l])
        for lst, t in zip(outs, (kp, vp, cp, ksn, vsn, csn)):
            lst.append(t)
    return (xp, xs) + tuple(jnp.stack(t) for t in outs)
```

```python
import functools

import jax
import jax.numpy as jnp
from jax import lax
from jax.experimental import pallas as pl
from jax.experimental.pallas import tpu as pltpu

N_HEADS = 8
HEAD_DIM = 64
ATT_DIM = N_HEADS * HEAD_DIM
CONV_WIDTH = 3
EPS = 1e-6

LANES = 128
HEADS_PER_GROUP = LANES // HEAD_DIM
N_GROUPS = ATT_DIM // LANES

PROJ_ROWS = 512
ATT_BLOCK = 256
ATT_QUERY_TILE = 2 * ATT_BLOCK
GROUPS_PER_BODY = 2
PAGES_PER_CHUNK = 8
CHUNKS_AHEAD = 3
KEY_SEGMENT = ATT_BLOCK
VMEM_LIMIT = 58 * 1024 * 1024

F32 = jnp.float32
BF16 = jnp.bfloat16
LOG2E = 1.4426950408889634
SOFTPLUS2_LINEAR_ABOVE = 64.0


def _rms(x, g):
    ms = jnp.mean(x * x, axis=-1, keepdims=True)
    return x * lax.rsqrt(ms + EPS) * g


def _softplus(z):
    return jnp.maximum(z, 0.0) + jnp.log(1.0 + jnp.exp(-jnp.abs(z)))


def _softplus2(z2):
    return jnp.where(z2 > SOFTPLUS2_LINEAR_ABOVE, z2,
                     jnp.log2(1.0 + jnp.exp2(z2)))


def _split_bf16(x):
    hi = x.astype(BF16)
    lo = (x - hi.astype(F32)).astype(BF16)
    return hi, lo


def _suffix_sum(sp, tri2):
    hi, lo = _split_bf16(sp)
    return jnp.dot(jnp.concatenate([hi, lo], axis=1), tri2,
                   preferred_element_type=F32)


def _ada_kernel(c_ref, w_ref, b_ref, o_ref):
    c = c_ref[...]
    a = c * jax.nn.sigmoid(c)
    o_ref[...] = jnp.dot(a, w_ref[...], preferred_element_type=F32,
                         precision=lax.Precision.HIGHEST) + b_ref[...]


def _ada(c_all, w_ada, b_ada):
    rows = c_all.shape[0]
    return pl.pallas_call(
        _ada_kernel,
        out_shape=jax.ShapeDtypeStruct((rows, w_ada.shape[1]), F32),
        compiler_params=pltpu.CompilerParams(vmem_limit_bytes=VMEM_LIMIT),
        name="ada",
    )(c_all, w_ada, b_ada)


def _split_mod(mod):
    d = mod.shape[-1] // 3
    return mod[:, 0:d], mod[:, d:2 * d], mod[:, 2 * d:3 * d]


def _modulated_norm(x, mod, g_pre):
    shift, scale, _ = _split_mod(mod)
    return (_rms(x, g_pre) * (1.0 + scale) + shift).astype(BF16)


def _conv_branch(u, u1, u2, bc, gc, conv_w, g_conv):
    conv_y = conv_w[0:1] * u2 + conv_w[1:2] * u1 + conv_w[2:3] * u
    return _rms(bc * conv_y, g_conv) * (gc * jax.nn.sigmoid(gc))


def _prompt_proj_kernel(x_ref, mod_ref, gpre_ref, w_ref, cw_ref, gconv_ref,
                        ya_ref, q_ref, k_ref, kb_ref, v_ref, vb_ref, ga_ref,
                        conv_ref, u_scr):
    rows = x_ref.shape[1]
    cdim = cw_ref.shape[1]
    h = _modulated_norm(x_ref[0], mod_ref[0], gpre_ref[...])

    def seg(i):
        return jnp.dot(h, w_ref[:, i * cdim:(i + 1) * cdim],
                       preferred_element_type=F32)

    @pl.when(pl.program_id(1) == 0)
    def _():
        u_scr[0:8, :] = jnp.zeros((8, cdim), F32)

    u = seg(2) * seg(0)
    u_scr[8:8 + rows, :] = u
    u1 = u_scr[7:7 + rows, :]
    u2 = u_scr[6:6 + rows, :]
    ya = _conv_branch(u, u1, u2, seg(1), seg(3), cw_ref[...], gconv_ref[...])
    ya_ref[0] = ya.astype(BF16)
    tail = u_scr[rows + 6:rows + 8, :]
    conv_ref[0] = tail
    u_scr[6:8, :] = tail

    q = (seg(4) * (LOG2E * HEAD_DIM ** -0.5)).astype(BF16)
    k = seg(5)
    v = seg(6)
    k_ref[0] = k
    v_ref[0] = v
    kb = k.astype(BF16)
    vb = v.astype(BF16)
    for g in range(N_GROUPS):
        sl = slice(g * LANES, (g + 1) * LANES)
        q_ref[0, g] = q[:, sl]
        kb_ref[0, g] = kb[:, sl]
        vb_ref[0, g] = vb[:, sl]
    ga = seg(7)
    ga_ref[0] = ga * jax.nn.sigmoid(ga)


def _prompt_proj(x, mod, g_pre, w_in_b, conv_w, g_conv):
    B, S, D = x.shape
    cdim = conv_w.shape[1]
    rows = PROJ_ROWS
    grid = (B, S // rows)
    row_blk = lambda w: pl.BlockSpec((1, rows, w), lambda b, s: (b, s, 0))
    grp_blk = pl.BlockSpec((1, N_GROUPS, rows, LANES), lambda b, s: (b, 0, s, 0))
    const = lambda shp: pl.BlockSpec(shp, lambda b, s: (0,) * len(shp))
    grp_shape = jax.ShapeDtypeStruct((B, N_GROUPS, S, LANES), BF16)
    return pl.pallas_call(
        _prompt_proj_kernel,
        grid=grid,
        in_specs=[row_blk(D),
                  pl.BlockSpec((1, 1, 3 * D), lambda b, s: (b, 0, 0)),
                  const((1, D)), const(w_in_b.shape), const(conv_w.shape),
                  const((1, cdim))],
        out_specs=[row_blk(cdim), grp_blk, row_blk(ATT_DIM), grp_blk,
                   row_blk(ATT_DIM), grp_blk, row_blk(ATT_DIM),
                   pl.BlockSpec((1, CONV_WIDTH - 1, cdim), lambda b, s: (b, 0, 0))],
        out_shape=[jax.ShapeDtypeStruct((B, S, cdim), BF16), grp_shape,
                   jax.ShapeDtypeStruct((B, S, ATT_DIM), F32), grp_shape,
                   jax.ShapeDtypeStruct((B, S, ATT_DIM), F32), grp_shape,
                   jax.ShapeDtypeStruct((B, S, ATT_DIM), F32),
                   jax.ShapeDtypeStruct((B, CONV_WIDTH - 1, cdim), F32)],
        scratch_shapes=[pltpu.VMEM((rows + 8, cdim), F32)],
        compiler_params=pltpu.CompilerParams(
            dimension_semantics=("arbitrary", "arbitrary"),
            vmem_limit_bytes=VMEM_LIMIT),
        name="prompt_proj",
    )(x, mod, g_pre, w_in_b, conv_w, g_conv)


def _mixer_out(o, sga, ya, x, gate, g_attn, w_out_ref, g_post):
    cdim = ya.shape[-1]
    yb = (_rms(o, g_attn) * sga).astype(BF16)
    m = (jnp.dot(ya, w_out_ref[0:cdim, :], preferred_element_type=F32)
         + jnp.dot(yb, w_out_ref[cdim:, :], preferred_element_type=F32))
    return x + gate * _rms(m, g_post)


def _decode_keys(kpages, qcol_ref, bias, tri2, carry_ref, fresh):
    n_pages, _, page_tokens = kpages.shape
    seg_pages = KEY_SEGMENT // page_tokens
    n_seg = n_pages // seg_pages
    sub = 8

    z_rows = [[None] * N_HEADS for _ in range(n_pages)]
    for h in range(N_HEADS):
        part = [None] * n_pages
        for r in range(HEAD_DIM // sub):
            rows = slice(h * HEAD_DIM + r * sub, h * HEAD_DIM + (r + 1) * sub)
            q = qcol_ref[rows, :]
            for p in range(n_pages):
                t = kpages[p, rows, :] * q
                part[p] = t if r == 0 else part[p] + t
        for p in range(n_pages):
            z_rows[p][h] = jnp.sum(part[p], axis=0, keepdims=True)
    z_seg, sp_seg = [], []
    for s in range(n_seg):
        z = jnp.concatenate(
            [jnp.concatenate(z_rows[p], axis=0)
             for p in range(s * seg_pages, (s + 1) * seg_pages)], axis=1)
        z = z + jnp.concatenate([bias] * seg_pages, axis=1)
        z_seg.append(z)
        sp_seg.append(_softplus(z))

    suffix = _suffix_sum(jnp.concatenate(sp_seg, axis=0), tri2)

    carry = jnp.where(fresh, 0.0, carry_ref[...])
    a_seg = [None] * n_seg
    for s in reversed(range(n_seg)):
        c = jnp.concatenate([carry] * seg_pages, axis=1)
        a_seg[s] = jnp.exp(z_seg[s] - suffix[s * N_HEADS:(s + 1) * N_HEADS] - c)
        total = jnp.sum(sp_seg[s], axis=1, keepdims=True)
        carry = carry + jnp.broadcast_to(total, carry.shape)
    carry_ref[...] = carry
    return jnp.concatenate(a_seg, axis=1)


def _decode_values(vpages, a_ref, acc_ref, fresh):
    n_pages, _, page_tokens = vpages.shape
    for h in range(N_HEADS):
        rows = slice(h * HEAD_DIM, (h + 1) * HEAD_DIM)
        acc = jnp.where(fresh, 0.0, acc_ref[rows, :])
        for p in range(n_pages):
            a_row = a_ref[h:h + 1, p * page_tokens:(p + 1) * page_tokens]
            acc = acc + vpages[p, rows, :] * a_row
        acc_ref[rows, :] = acc


def _attn_kernel(pt_ref, b_ref, q_ref, k_ref, v_ref, tri_ref, ya_ref, sga_ref,
                 x_ref, mod_ref, gattn_ref, wout_ref, gpost_ref,
                 qcol_hbm, bcol_ref, ck_hbm, cv_hbm,
                 y_ref, os_ref,
                 qm_scr, z_scr, a_scr, acc_scr, carry_scr, o_scr,
                 kbuf, vbuf, qbuf, ksem, vsem, qsem, da_scr, dacc_scr, dcarry_scr,
                 chunk_ref, *,
                 extra_chunks):
    blk = ATT_BLOCK
    halves = q_ref.shape[2] // blk
    chains = range(HEADS_PER_GROUP * halves)
    i = pl.program_id(1)
    lane = lax.broadcasted_iota(jnp.int32, (blk, LANES), 1)
    row = lax.broadcasted_iota(jnp.int32, (blk, blk), 0)
    col = lax.broadcasted_iota(jnp.int32, (blk, blk), 1)
    causal = col < row
    tri2 = tri_ref[...]

    n_seq, n_pages = pt_ref.shape
    groups = n_pages // PAGES_PER_CHUNK
    n_chunks = n_seq * groups

    def chunk_copies(c):
        kslot = c % kbuf.shape[0]
        vslot = c % vbuf.shape[0]
        seq = c // groups
        first = (groups - 1 - c % groups) * PAGES_PER_CHUNK
        out = []
        for p in range(PAGES_PER_CHUNK):
            page = pt_ref[seq, first + p]
            out.append(pltpu.make_async_copy(ck_hbm.at[page], kbuf.at[kslot, p],
                                             ksem.at[kslot]))
            out.append(pltpu.make_async_copy(cv_hbm.at[page], vbuf.at[vslot, p],
                                             vsem.at[vslot]))
        return out

    def query_copy(c):
        seq = c // groups
        return pltpu.make_async_copy(qcol_hbm.at[seq], qbuf.at[seq % 2],
                                     qsem.at[seq % 2])

    def start_chunk(c):
        for cp in chunk_copies(c):
            cp.start()

        @pl.when(c % groups == 0)
        def _():
            query_copy(c).start()

    def wait_chunk(c):
        for cp in chunk_copies(c):
            cp.wait()

        @pl.when(c % groups == 0)
        def _():
            query_copy(c).wait()

    def decode_begin():
        c = chunk_ref[0]

        @pl.when(c + CHUNKS_AHEAD < n_chunks)
        def _():
            start_chunk(c + CHUNKS_AHEAD)

        wait_chunk(c)
        return c

    def decode_values(c):
        _decode_values(vbuf.at[(c + vbuf.shape[0] - 1) % vbuf.shape[0]], da_scr,
                       dacc_scr, (c + groups - 1) % groups == 0)

    def finish_sequence(c):
        @pl.when((c % groups == 0) & (c > 0))
        def _():
            os_ref[pl.ds((c - 1) // groups, 1), :] = jnp.sum(dacc_scr[...].T, axis=0,
                                                             keepdims=True)

    def decode_keys(c):
        da_scr[...] = _decode_keys(kbuf.at[c % kbuf.shape[0]], qbuf.at[(c // groups) % 2],
                                   bcol_ref[...], tri2, dcarry_scr, c % groups == 0)

    def decode_compute(c):
        decode_values(c)
        decode_keys(c)

    def decode_end(c):
        finish_sequence(c)
        chunk_ref[0] = c + 1

    def decode_chunk():
        c = decode_begin()
        decode_compute(c)
        decode_end(c)

    first_step = (pl.program_id(0) == 0) & (i == 0)
    last_step = ((pl.program_id(0) == pl.num_programs(0) - 1)
                 & (i == pl.num_programs(1) - 1))

    @pl.when(first_step)
    def _():
        chunk_ref[0] = 0
        dacc_scr[...] = jnp.zeros_like(dacc_scr)
        dcarry_scr[...] = jnp.zeros_like(dcarry_scr)
        da_scr[...] = jnp.zeros_like(da_scr)
        vbuf[vbuf.shape[0] - 1] = jnp.zeros(vbuf.shape[1:], F32)
        for c in range(CHUNKS_AHEAD):
            start_chunk(jnp.int32(c))

    def raw_scores(p, g, key_blk, active):
        off = pl.multiple_of(key_blk * blk, blk)
        kb = k_ref[0, g, pl.ds(off, blk), :]
        qs = jnp.concatenate([qm_scr[p * len(chains) + c] for c in active], axis=0)
        return lax.dot_general(qs, kb, (((1,), (1,)), ((), ())),
                               preferred_element_type=F32)

    def weights(p, g, s, active, masked):
        base = p * len(chains)
        z, sp = [], []
        for n, c in enumerate(active):
            bias2 = b_ref[g * HEADS_PER_GROUP + c // halves] * LOG2E
            zc = s[n * blk:(n + 1) * blk] + bias2
            spc = _softplus2(zc)
            if c in masked:
                spc = jnp.where(causal, spc, 0.0)
            z.append(zc)
            sp.append(spc)
        suffix = jnp.dot(jnp.concatenate(sp, axis=0).astype(BF16), tri2[0:blk],
                         preferred_element_type=F32)
        a = []
        for n, c in enumerate(active):
            carry = carry_scr[base + c]
            c2 = jnp.concatenate([carry] * (blk // LANES), axis=1)
            ac = jnp.exp2(z[n] - suffix[n * blk:(n + 1) * blk] - c2)
            if c in masked:
                ac = jnp.where(causal, ac, 0.0)
            a.append(ac.astype(BF16))
            total = jnp.sum(sp[n], axis=1, keepdims=True)
            carry_scr[base + c] = carry + jnp.broadcast_to(total, (blk, LANES))
        return jnp.concatenate(a, axis=0)

    def accumulate(p, g, key_blk, a, active):
        base = p * len(chains)
        off = pl.multiple_of(key_blk * blk, blk)
        vb = v_ref[0, g, pl.ds(off, blk), :]
        o = jnp.dot(a, vb, preferred_element_type=F32)
        for n, c in enumerate(active):
            acc_scr[base + c] += o[n * blk:(n + 1) * blk]

    def group_body(gg, _):
        groups_here = [(p, gg * GROUPS_PER_BODY + p) for p in range(GROUPS_PER_BODY)]
        for p, g in groups_here:
            qg = q_ref[0, g]
            for c in chains:
                hh, half = divmod(c, halves)
                in_head = (lane >= hh * HEAD_DIM) & (lane < (hh + 1) * HEAD_DIM)
                qh = qg[half * blk:(half + 1) * blk]
                qm_scr[p * len(chains) + c] = jnp.where(in_head, qh, jnp.zeros_like(qh))
        acc_scr[...] = jnp.zeros_like(acc_scr)
        carry_scr[...] = jnp.zeros_like(carry_scr)

        top = halves * i
        everyone = list(chains)
        for p, g in groups_here:
            z_scr[p, 0] = raw_scores(p, g, jnp.maximum(top - 1, 0), everyone)

        for d in reversed(range(1, halves)):
            active = [c for c in chains if c % halves >= d]
            masked = {c for c in active if c % halves == d}
            for p, g in groups_here:
                a = weights(p, g, raw_scores(p, g, top + d, active), active, masked)
                accumulate(p, g, top + d, a, active)
        for p, g in groups_here:
            a_scr[p, 1] = weights(p, g, raw_scores(p, g, top, everyone), everyone,
                                  {c for c in chains if c % halves == 0})

        @pl.when(gg < extra_chunks)
        def _():
            decode_chunk()

        assert halves == 2

        def key_body(m, _, p, g):
            nxt = top - 1 - halves * m
            chunk = decode_begin()
            decode_parts = [decode_values, decode_keys]
            for cur in range(halves):
                accumulate(p, g, nxt - cur + 1, a_scr[p, 1 - cur], everyone)
                z_scr[p, 1 - cur] = raw_scores(p, g, jnp.maximum(nxt - cur - 1, 0),
                                               everyone)
                a_scr[p, cur] = weights(p, g, z_scr[p, cur], everyone, set())
                decode_parts[cur](chunk)
            decode_end(chunk)
            return 0

        for p, g in groups_here:
            lax.fori_loop(0, i, functools.partial(key_body, p=p, g=g), 0)
        for p, g in groups_here:
            accumulate(p, g, 0, a_scr[p, 1], everyone)
            base = p * len(chains)
            heads = [jnp.concatenate([acc_scr[base + hh * halves + half]
                                      for half in range(halves)], axis=0)
                     for hh in range(HEADS_PER_GROUP)]
            lane_q = lax.broadcasted_iota(jnp.int32, heads[0].shape, 1)
            o_scr[g] = jnp.where(lane_q < HEAD_DIM, heads[0], heads[1])
        return 0

    lax.fori_loop(0, N_GROUPS // GROUPS_PER_BODY, group_body, 0)

    o = jnp.concatenate([o_scr[g] for g in range(N_GROUPS)], axis=1)
    y_ref[0] = _mixer_out(o, sga_ref[0], ya_ref[0], x_ref[0],
                          _split_mod(mod_ref[0])[2], gattn_ref[...], wout_ref,
                          gpost_ref[...])

    @pl.when(last_step)
    def _():
        def drain(_, carry):
            decode_chunk()
            return carry

        lax.fori_loop(chunk_ref[0], n_chunks, drain, 0)
        decode_values(n_chunks)
        finish_sequence(n_chunks)


def _attention(page_table, b_sb, q4, k4, v4, tri2, ya, sga, x, mod, g_attn, w_out_b,
               g_post, qcol, bias_col, cache_k, cache_v):
    B, S, D = x.shape
    n_seq, n_pages = page_table.shape
    page_tokens = cache_k.shape[2]
    assert n_pages % PAGES_PER_CHUNK == 0 and KEY_SEGMENT % page_tokens == 0
    blk = ATT_QUERY_TILE
    n_chain = HEADS_PER_GROUP * (blk // ATT_BLOCK)
    cdim = ya.shape[-1]
    n_tiles = S // blk
    n_chunks = n_seq * (n_pages // PAGES_PER_CHUNK)
    bodies = N_GROUPS // GROUPS_PER_BODY
    trips = B * N_GROUPS * sum(range(n_tiles))
    assert trips <= n_chunks, "more key-loop trips than decode chunks"
    extra_chunks = min(bodies, (n_chunks - trips) // (B * n_tiles))
    row_blk = lambda w: pl.BlockSpec((1, blk, w), lambda b, i, pt: (b, i, 0))
    once = pl.Buffered(1)
    const = lambda shp: pl.BlockSpec(shp, lambda b, i, pt: (0,) * len(shp),
                                     pipeline_mode=once)
    whole_seq = pl.BlockSpec((1, N_GROUPS, S, LANES), lambda b, i, pt: (b, 0, 0, 0),
                             pipeline_mode=once)
    page_buf = lambda slots: pltpu.VMEM(
        (slots, PAGES_PER_CHUNK, ATT_DIM, page_tokens), F32)
    grid_spec = pltpu.PrefetchScalarGridSpec(
        num_scalar_prefetch=1,
        grid=(B, S // blk),
        in_specs=[pl.BlockSpec(memory_space=pltpu.SMEM),
                  pl.BlockSpec((1, N_GROUPS, blk, LANES), lambda b, i, pt: (b, 0, i, 0)),
                  whole_seq, whole_seq, const(tri2.shape),
                  row_blk(cdim), row_blk(ATT_DIM), row_blk(D),
                  pl.BlockSpec((1, 1, 3 * D), lambda b, i, pt: (b, 0, 0)),
                  const((1, ATT_DIM)), const(w_out_b.shape), const((1, D)),
                  pl.BlockSpec(memory_space=pl.ANY),
                  const(bias_col.shape),
                  pl.BlockSpec(memory_space=pl.ANY),
                  pl.BlockSpec(memory_space=pl.ANY)],
        out_specs=[row_blk(D),
                   pl.BlockSpec((n_seq, ATT_DIM), lambda b, i, pt: (0, 0))],
        scratch_shapes=[pltpu.VMEM((GROUPS_PER_BODY * n_chain, ATT_BLOCK, LANES), BF16),
                        pltpu.VMEM((GROUPS_PER_BODY, 2, n_chain * ATT_BLOCK, ATT_BLOCK), F32),
                        pltpu.VMEM((GROUPS_PER_BODY, 2, n_chain * ATT_BLOCK, ATT_BLOCK), BF16),
                        pltpu.VMEM((GROUPS_PER_BODY * n_chain, ATT_BLOCK, LANES), F32),
                        pltpu.VMEM((GROUPS_PER_BODY * n_chain, ATT_BLOCK, LANES), F32),
                        pltpu.VMEM((N_GROUPS, blk, LANES), F32),
                        page_buf(CHUNKS_AHEAD + 1), page_buf(CHUNKS_AHEAD + 2),
                        pltpu.VMEM((2, ATT_DIM, LANES), F32),
                        pltpu.SemaphoreType.DMA((CHUNKS_AHEAD + 1,)),
                        pltpu.SemaphoreType.DMA((CHUNKS_AHEAD + 2,)),
                        pltpu.SemaphoreType.DMA((2,)),
                        pltpu.VMEM((N_HEADS, PAGES_PER_CHUNK * page_tokens), F32),
                        pltpu.VMEM((ATT_DIM, LANES), F32),
                        pltpu.VMEM((N_HEADS, LANES), F32),
                        pltpu.SMEM((1,), jnp.int32)],
    )
    return pl.pallas_call(
        functools.partial(_attn_kernel, extra_chunks=extra_chunks),
        grid_spec=grid_spec,
        out_shape=[jax.ShapeDtypeStruct((B, S, D), F32),
                   jax.ShapeDtypeStruct((n_seq, ATT_DIM), F32)],
        compiler_params=pltpu.CompilerParams(
            dimension_semantics=("arbitrary", "arbitrary"),
            vmem_limit_bytes=VMEM_LIMIT),
        name="attention",
    )(page_table, b_sb, q4, k4, v4, tri2, ya, sga, x, mod, g_attn, w_out_b, g_post,
      qcol, bias_col, cache_k, cache_v)


def _sample_proj_kernel(x_ref, mod_ref, st_ref, gpre_ref, w_ref, cw_ref,
                        gconv_ref, ya_ref, q_ref, k_ref, v_ref, ga_ref, conv_ref):
    cdim = cw_ref.shape[1]
    h = _modulated_norm(x_ref[...], mod_ref[...], gpre_ref[...])
    p = jnp.dot(h, w_ref[...], preferred_element_type=F32)
    seg = lambda i: p[:, i * cdim:(i + 1) * cdim]
    u = seg(2) * seg(0)
    u2, u1 = st_ref[:, 0:cdim], st_ref[:, cdim:2 * cdim]
    ya_ref[...] = _conv_branch(u, u1, u2, seg(1), seg(3), cw_ref[...],
                               gconv_ref[...]).astype(BF16)
    conv_ref[:, 0:cdim] = u1
    conv_ref[:, cdim:2 * cdim] = u
    q_ref[...] = seg(4) * (HEAD_DIM ** -0.5)
    k_ref[...] = seg(5)
    v_ref[...] = seg(6)
    ga = seg(7)
    ga_ref[...] = ga * jax.nn.sigmoid(ga)


def _sample_proj(x, mod, state, g_pre, w_in_b, conv_w, g_conv):
    n, _ = x.shape
    cdim = conv_w.shape[1]
    sds = lambda w, dt=F32: jax.ShapeDtypeStruct((n, w), dt)
    return pl.pallas_call(
        _sample_proj_kernel,
        out_shape=[sds(cdim, BF16), sds(ATT_DIM), sds(ATT_DIM), sds(ATT_DIM),
                   sds(ATT_DIM), sds((CONV_WIDTH - 1) * cdim)],
        compiler_params=pltpu.CompilerParams(vmem_limit_bytes=VMEM_LIMIT),
        name="sample_proj",
    )(x, mod, state, g_pre, w_in_b, conv_w, g_conv)


def _sample_out_kernel(o_ref, sga_ref, ya_ref, x_ref, mod_ref, gattn_ref,
                       wout_ref, gpost_ref, y_ref):
    y_ref[...] = _mixer_out(o_ref[...], sga_ref[...], ya_ref[...], x_ref[...],
                            _split_mod(mod_ref[...])[2], gattn_ref[...], wout_ref,
                            gpost_ref[...])


def _sample_out(o, sga, ya, x, mod, g_attn, w_out_b, g_post):
    return pl.pallas_call(
        _sample_out_kernel,
        out_shape=jax.ShapeDtypeStruct(x.shape, F32),
        compiler_params=pltpu.CompilerParams(vmem_limit_bytes=VMEM_LIMIT),
        name="sample_out",
    )(o, sga, ya, x, mod, g_attn, w_out_b, g_post)


def _layer(xp, xs, c_all, pool_k, pool_v, state, page_table, w_ada, b_ada, g_pre,
           w_in, conv_w, g_conv, b_sb, g_attn, w_out, g_post):
    B, S, D = xp.shape
    n_seq = xs.shape[0]
    row = lambda v: v.reshape(1, -1)
    w_in_b = w_in.astype(BF16)
    w_out_b = w_out.astype(BF16)

    mod = _ada(c_all, w_ada, row(b_ada))
    mod_p, mod_s = mod[:B].reshape(B, 1, 3 * D), mod[B:B + n_seq]

    ya, q4, k, kb4, v, vb4, sga, conv_p = _prompt_proj(
        xp, mod_p, row(g_pre), w_in_b, conv_w, row(g_conv))
    xs2 = xs.reshape(n_seq, D)
    ya_s, q_s, k_s, v_s, sga_s, conv_s = _sample_proj(
        xs2, mod_s, state.reshape(n_seq, -1), row(g_pre), w_in_b, conv_w,
        row(g_conv))
    conv_s = conv_s.reshape(state.shape)

    idx = jnp.arange(ATT_BLOCK)
    tri = (idx[:, None] >= idx[None, :]).astype(BF16)
    tri2 = jnp.concatenate([tri, tri], axis=0)
    n_pool, page_tokens = pool_k.shape[:2]
    assert page_tokens == LANES
    pages = lambda pool: jnp.transpose(pool, (0, 2, 3, 1)).reshape(
        n_pool, ATT_DIM, page_tokens)
    qcol = jnp.broadcast_to(q_s[:, :, None], (n_seq, ATT_DIM, LANES))
    bias_col = jnp.broadcast_to(b_sb[:, None], (N_HEADS, LANES))
    yp, o_s = _attention(page_table, b_sb, q4, kb4, vb4, tri2, ya, sga, xp, mod_p,
                         row(g_attn), w_out_b, row(g_post), qcol, bias_col,
                         pages(pool_k), pages(pool_v))
    ys = _sample_out(o_s, sga_s, ya_s, xs2, mod_s, row(g_attn), w_out_b, row(g_post))

    heads = lambda t, n, L: t.reshape(n, L, N_HEADS, HEAD_DIM)
    return (yp, ys.reshape(xs.shape), heads(k, B, S), heads(v, B, S), conv_p,
            heads(k_s, n_seq, 1), heads(v_s, n_seq, 1), conv_s)


def kernel(x_prompt, x_sample, c_prompt, c_sample, cache_k, cache_v, state_conv,
           page_table, w_ada, b_ada, g_pre, w_in, conv_w, g_conv, b_sb, g_attn,
           w_out, g_post):
    assert x_sample.shape[1] == 1, "decode step handles one new token per sequence"
    depth = w_in.shape[0]
    n_rows = c_prompt.shape[0] + c_sample.shape[0]
    pad = (-n_rows) % 8
    c_all = jnp.concatenate(
        [c_prompt, c_sample, jnp.zeros((pad, c_prompt.shape[1]), F32)], axis=0)
    xp, xs = x_prompt, x_sample
    outs = [[] for _ in range(6)]
    for l in range(depth):
        xp, xs, kp, vp, cp, ksn, vsn, csn = _layer(
            xp, xs, c_all, cache_k[l], cache_v[l], state_conv[l], page_table,
            w_ada[l], b_ada[l], g_pre[l], w_in[l], conv_w[l], g_conv[l], b_sb[l],
            g_attn[l], w_out[l], g_post[l])
        for lst, t in zip(outs, (kp, vp, cp, ksn, vsn, csn)):
            lst.append(t)
    return (xp, xs) + tuple(jnp.stack(t) for t in outs)
```

```python
import functools

import jax
import jax.numpy as jnp
from jax import lax
from jax.experimental import pallas as pl
from jax.experimental.pallas import tpu as pltpu

N_HEADS = 8
HEAD_DIM = 64
ATT_DIM = N_HEADS * HEAD_DIM
CONV_WIDTH = 3
EPS = 1e-6

LANES = 128
HEADS_PER_GROUP = LANES // HEAD_DIM
N_GROUPS = ATT_DIM // LANES

PROJ_ROWS = 512
ATT_BLOCK = 256
ATT_QUERY_TILE = 2 * ATT_BLOCK
GROUPS_PER_BODY = 2
PAGES_PER_CHUNK = 8
CHUNKS_AHEAD = 2
KEY_SEGMENT = ATT_BLOCK
VMEM_LIMIT = 56 * 1024 * 1024

F32 = jnp.float32
BF16 = jnp.bfloat16
LOG2E = 1.4426950408889634
SOFTPLUS2_LINEAR_ABOVE = 64.0


def _rms(x, g):
    ms = jnp.mean(x * x, axis=-1, keepdims=True)
    return x * lax.rsqrt(ms + EPS) * g


def _softplus(z):
    return jnp.maximum(z, 0.0) + jnp.log(1.0 + jnp.exp(-jnp.abs(z)))


def _softplus2(z2):
    return jnp.where(z2 > SOFTPLUS2_LINEAR_ABOVE, z2,
                     jnp.log2(1.0 + jnp.exp2(z2)))


def _split_bf16(x):
    hi = x.astype(BF16)
    lo = (x - hi.astype(F32)).astype(BF16)
    return hi, lo


def _suffix_sum(sp, tri2):
    hi, lo = _split_bf16(sp)
    return jnp.dot(jnp.concatenate([hi, lo], axis=1), tri2,
                   preferred_element_type=F32)


def _ada_kernel(c_ref, w_ref, b_ref, o_ref):
    c = c_ref[...]
    a = c * jax.nn.sigmoid(c)
    o_ref[...] = jnp.dot(a, w_ref[...], preferred_element_type=F32,
                         precision=lax.Precision.HIGHEST) + b_ref[...]


def _ada(c_all, w_ada, b_ada):
    rows = c_all.shape[0]
    return pl.pallas_call(
        _ada_kernel,
        out_shape=jax.ShapeDtypeStruct((rows, w_ada.shape[1]), F32),
        compiler_params=pltpu.CompilerParams(vmem_limit_bytes=VMEM_LIMIT),
        name="ada",
    )(c_all, w_ada, b_ada)


def _split_mod(mod):
    d = mod.shape[-1] // 3
    return mod[:, 0:d], mod[:, d:2 * d], mod[:, 2 * d:3 * d]


def _modulated_norm(x, mod, g_pre):
    shift, scale, _ = _split_mod(mod)
    return (_rms(x, g_pre) * (1.0 + scale) + shift).astype(BF16)


def _conv_branch(u, u1, u2, bc, gc, conv_w, g_conv):
    conv_y = conv_w[0:1] * u2 + conv_w[1:2] * u1 + conv_w[2:3] * u
    return _rms(bc * conv_y, g_conv) * (gc * jax.nn.sigmoid(gc))


def _prompt_proj_kernel(x_ref, mod_ref, gpre_ref, w_ref, cw_ref, gconv_ref,
                        ya_ref, q_ref, k_ref, kb_ref, v_ref, vb_ref, ga_ref,
                        conv_ref, u_scr):
    rows = x_ref.shape[1]
    cdim = cw_ref.shape[1]
    h = _modulated_norm(x_ref[0], mod_ref[0], gpre_ref[...])

    def seg(i):
        return jnp.dot(h, w_ref[:, i * cdim:(i + 1) * cdim],
                       preferred_element_type=F32)

    @pl.when(pl.program_id(1) == 0)
    def _():
        u_scr[0:8, :] = jnp.zeros((8, cdim), F32)

    u = seg(2) * seg(0)
    u_scr[8:8 + rows, :] = u
    u1 = u_scr[7:7 + rows, :]
    u2 = u_scr[6:6 + rows, :]
    ya = _conv_branch(u, u1, u2, seg(1), seg(3), cw_ref[...], gconv_ref[...])
    ya_ref[0] = ya.astype(BF16)
    tail = u_scr[rows + 6:rows + 8, :]
    conv_ref[0] = tail
    u_scr[6:8, :] = tail

    q = (seg(4) * (LOG2E * HEAD_DIM ** -0.5)).astype(BF16)
    k = seg(5)
    v = seg(6)
    k_ref[0] = k
    v_ref[0] = v
    kb = k.astype(BF16)
    vb = v.astype(BF16)
    for g in range(N_GROUPS):
        sl = slice(g * LANES, (g + 1) * LANES)
        q_ref[0, g] = q[:, sl]
        kb_ref[0, g] = kb[:, sl]
        vb_ref[0, g] = vb[:, sl]
    ga = seg(7)
    ga_ref[0] = ga * jax.nn.sigmoid(ga)


def _prompt_proj(x, mod, g_pre, w_in_b, conv_w, g_conv):
    B, S, D = x.shape
    cdim = conv_w.shape[1]
    rows = PROJ_ROWS
    grid = (B, S // rows)
    row_blk = lambda w: pl.BlockSpec((1, rows, w), lambda b, s: (b, s, 0))
    grp_blk = pl.BlockSpec((1, N_GROUPS, rows, LANES), lambda b, s: (b, 0, s, 0))
    const = lambda shp: pl.BlockSpec(shp, lambda b, s: (0,) * len(shp))
    grp_shape = jax.ShapeDtypeStruct((B, N_GROUPS, S, LANES), BF16)
    return pl.pallas_call(
        _prompt_proj_kernel,
        grid=grid,
        in_specs=[row_blk(D),
                  pl.BlockSpec((1, 1, 3 * D), lambda b, s: (b, 0, 0)),
                  const((1, D)), const(w_in_b.shape), const(conv_w.shape),
                  const((1, cdim))],
        out_specs=[row_blk(cdim), grp_blk, row_blk(ATT_DIM), grp_blk,
                   row_blk(ATT_DIM), grp_blk, row_blk(ATT_DIM),
                   pl.BlockSpec((1, CONV_WIDTH - 1, cdim), lambda b, s: (b, 0, 0))],
        out_shape=[jax.ShapeDtypeStruct((B, S, cdim), BF16), grp_shape,
                   jax.ShapeDtypeStruct((B, S, ATT_DIM), F32), grp_shape,
                   jax.ShapeDtypeStruct((B, S, ATT_DIM), F32), grp_shape,
                   jax.ShapeDtypeStruct((B, S, ATT_DIM), F32),
                   jax.ShapeDtypeStruct((B, CONV_WIDTH - 1, cdim), F32)],
        scratch_shapes=[pltpu.VMEM((rows + 8, cdim), F32)],
        compiler_params=pltpu.CompilerParams(
            dimension_semantics=("arbitrary", "arbitrary"),
            vmem_limit_bytes=VMEM_LIMIT),
        name="prompt_proj",
    )(x, mod, g_pre, w_in_b, conv_w, g_conv)


def _mixer_out(o, sga, ya, x, gate, g_attn, w_out_ref, g_post):
    cdim = ya.shape[-1]
    yb = (_rms(o, g_attn) * sga).astype(BF16)
    m = (jnp.dot(ya, w_out_ref[0:cdim, :], preferred_element_type=F32)
         + jnp.dot(yb, w_out_ref[cdim:, :], preferred_element_type=F32))
    return x + gate * _rms(m, g_post)


def _decode_keys(kpages, qcol_ref, bias, tri2, carry_ref, fresh):
    n_pages, _, page_tokens = kpages.shape
    seg_pages = KEY_SEGMENT // page_tokens
    n_seg = n_pages // seg_pages
    sub = 8

    z_rows = [[None] * N_HEADS for _ in range(n_pages)]
    for h in range(N_HEADS):
        part = [None] * n_pages
        for r in range(HEAD_DIM // sub):
            rows = slice(h * HEAD_DIM + r * sub, h * HEAD_DIM + (r + 1) * sub)
            q = qcol_ref[rows, :]
            for p in range(n_pages):
                t = kpages[p, rows, :] * q
                part[p] = t if r == 0 else part[p] + t
        for p in range(n_pages):
            z_rows[p][h] = jnp.sum(part[p], axis=0, keepdims=True)
    z_seg, sp_seg = [], []
    for s in range(n_seg):
        z = jnp.concatenate(
            [jnp.concatenate(z_rows[p], axis=0)
             for p in range(s * seg_pages, (s + 1) * seg_pages)], axis=1)
        z = z + jnp.concatenate([bias] * seg_pages, axis=1)
        z_seg.append(z)
        sp_seg.append(_softplus(z))

    suffix = _suffix_sum(jnp.concatenate(sp_seg, axis=0), tri2)

    carry = jnp.where(fresh, 0.0, carry_ref[...])
    a_seg = [None] * n_seg
    for s in reversed(range(n_seg)):
        c = jnp.concatenate([carry] * seg_pages, axis=1)
        a_seg[s] = jnp.exp(z_seg[s] - suffix[s * N_HEADS:(s + 1) * N_HEADS] - c)
        total = jnp.sum(sp_seg[s], axis=1, keepdims=True)
        carry = carry + jnp.broadcast_to(total, carry.shape)
    carry_ref[...] = carry
    return jnp.concatenate(a_seg, axis=1)


def _decode_values(vpages, a_ref, acc_ref, fresh):
    n_pages, _, page_tokens = vpages.shape
    for h in range(N_HEADS):
        rows = slice(h * HEAD_DIM, (h + 1) * HEAD_DIM)
        acc = jnp.where(fresh, 0.0, acc_ref[rows, :])
        for p in range(n_pages):
            a_row = a_ref[h:h + 1, p * page_tokens:(p + 1) * page_tokens]
            acc = acc + vpages[p, rows, :] * a_row
        acc_ref[rows, :] = acc


def _attn_kernel(pt_ref, b_ref, q_ref, k_ref, v_ref, tri_ref, ya_ref, sga_ref,
                 x_ref, mod_ref, gattn_ref, wout_ref, gpost_ref,
                 qcol_hbm, bcol_ref, ck_hbm, cv_hbm,
                 y_ref, os_ref,
                 qm_scr, z_scr, a_scr, acc_scr, carry_scr, o_scr,
                 kbuf, vbuf, qbuf, ksem, vsem, qsem, da_scr, dacc_scr, dcarry_scr,
                 chunk_ref, *,
                 extra_chunks):
    blk = ATT_BLOCK
    halves = q_ref.shape[2] // blk
    chains = range(HEADS_PER_GROUP * halves)
    i = pl.program_id(1)
    lane = lax.broadcasted_iota(jnp.int32, (blk, LANES), 1)
    row = lax.broadcasted_iota(jnp.int32, (blk, blk), 0)
    col = lax.broadcasted_iota(jnp.int32, (blk, blk), 1)
    causal = col < row
    tri2 = tri_ref[...]

    n_seq, n_pages = pt_ref.shape
    groups = n_pages // PAGES_PER_CHUNK
    n_chunks = n_seq * groups

    def chunk_copies(c):
        kslot = c % kbuf.shape[0]
        vslot = c % vbuf.shape[0]
        seq = c // groups
        first = (groups - 1 - c % groups) * PAGES_PER_CHUNK
        out = []
        for p in range(PAGES_PER_CHUNK):
            page = pt_ref[seq, first + p]
            out.append(pltpu.make_async_copy(ck_hbm.at[page], kbuf.at[kslot, p],
                                             ksem.at[kslot]))
            out.append(pltpu.make_async_copy(cv_hbm.at[page], vbuf.at[vslot, p],
                                             vsem.at[vslot]))
        return out

    def query_copy(c):
        seq = c // groups
        return pltpu.make_async_copy(qcol_hbm.at[seq], qbuf.at[seq % 2],
                                     qsem.at[seq % 2])

    def start_chunk(c):
        for cp in chunk_copies(c):
            cp.start()

        @pl.when(c % groups == 0)
        def _():
            query_copy(c).start()

    def wait_chunk(c):
        for cp in chunk_copies(c):
            cp.wait()

        @pl.when(c % groups == 0)
        def _():
            query_copy(c).wait()

    def decode_begin():
        c = chunk_ref[0]

        @pl.when(c + CHUNKS_AHEAD < n_chunks)
        def _():
            start_chunk(c + CHUNKS_AHEAD)

        wait_chunk(c)
        return c

    def decode_values(c):
        _decode_values(vbuf.at[(c + vbuf.shape[0] - 1) % vbuf.shape[0]], da_scr,
                       dacc_scr, (c + groups - 1) % groups == 0)

    def finish_sequence(c):
        @pl.when((c % groups == 0) & (c > 0))
        def _():
            os_ref[pl.ds((c - 1) // groups, 1), :] = jnp.sum(dacc_scr[...].T, axis=0,
                                                             keepdims=True)

    def decode_keys(c):
        da_scr[...] = _decode_keys(kbuf.at[c % kbuf.shape[0]], qbuf.at[(c // groups) % 2],
                                   bcol_ref[...], tri2, dcarry_scr, c % groups == 0)

    def decode_compute(c):
        decode_values(c)
        decode_keys(c)

    def decode_end(c):
        finish_sequence(c)
        chunk_ref[0] = c + 1

    def decode_chunk():
        c = decode_begin()
        decode_compute(c)
        decode_end(c)

    first_step = (pl.program_id(0) == 0) & (i == 0)
    last_step = ((pl.program_id(0) == pl.num_programs(0) - 1)
                 & (i == pl.num_programs(1) - 1))

    @pl.when(first_step)
    def _():
        chunk_ref[0] = 0
        dacc_scr[...] = jnp.zeros_like(dacc_scr)
        dcarry_scr[...] = jnp.zeros_like(dcarry_scr)
        da_scr[...] = jnp.zeros_like(da_scr)
        vbuf[vbuf.shape[0] - 1] = jnp.zeros(vbuf.shape[1:], F32)
        for c in range(CHUNKS_AHEAD):
            start_chunk(jnp.int32(c))

    def raw_scores(p, g, key_blk, active):
        off = pl.multiple_of(key_blk * blk, blk)
        kb = k_ref[0, g, pl.ds(off, blk), :]
        qs = jnp.concatenate([qm_scr[p * len(chains) + c] for c in active], axis=0)
        return lax.dot_general(qs, kb, (((1,), (1,)), ((), ())),
                               preferred_element_type=F32)

    def weights(p, g, s, active, masked):
        base = p * len(chains)
        z, sp = [], []
        for n, c in enumerate(active):
            bias2 = b_ref[g * HEADS_PER_GROUP + c // halves] * LOG2E
            zc = s[n * blk:(n + 1) * blk] + bias2
            spc = _softplus2(zc)
            if c in masked:
                spc = jnp.where(causal, spc, 0.0)
            z.append(zc)
            sp.append(spc)
        suffix = jnp.dot(jnp.concatenate(sp, axis=0).astype(BF16), tri2[0:blk],
                         preferred_element_type=F32)
        a = []
        for n, c in enumerate(active):
            carry = carry_scr[base + c]
            c2 = jnp.concatenate([carry] * (blk // LANES), axis=1)
            ac = jnp.exp2(z[n] - suffix[n * blk:(n + 1) * blk] - c2)
            if c in masked:
                ac = jnp.where(causal, ac, 0.0)
            a.append(ac.astype(BF16))
            total = jnp.sum(sp[n], axis=1, keepdims=True)
            carry_scr[base + c] = carry + jnp.broadcast_to(total, (blk, LANES))
        return jnp.concatenate(a, axis=0)

    def accumulate(p, g, key_blk, a, active):
        base = p * len(chains)
        off = pl.multiple_of(key_blk * blk, blk)
        vb = v_ref[0, g, pl.ds(off, blk), :]
        o = jnp.dot(a, vb, preferred_element_type=F32)
        for n, c in enumerate(active):
            acc_scr[base + c] += o[n * blk:(n + 1) * blk]

    def group_body(gg, _):
        groups_here = [(p, gg * GROUPS_PER_BODY + p) for p in range(GROUPS_PER_BODY)]
        for p, g in groups_here:
            qg = q_ref[0, g]
            for c in chains:
                hh, half = divmod(c, halves)
                in_head = (lane >= hh * HEAD_DIM) & (lane < (hh + 1) * HEAD_DIM)
                qh = qg[half * blk:(half + 1) * blk]
                qm_scr[p * len(chains) + c] = jnp.where(in_head, qh, jnp.zeros_like(qh))
        acc_scr[...] = jnp.zeros_like(acc_scr)
        carry_scr[...] = jnp.zeros_like(carry_scr)

        top = halves * i
        everyone = list(chains)
        for p, g in groups_here:
            z_scr[p, 0] = raw_scores(p, g, jnp.maximum(top - 1, 0), everyone)

        for d in reversed(range(1, halves)):
            active = [c for c in chains if c % halves >= d]
            masked = {c for c in active if c % halves == d}
            for p, g in groups_here:
                a = weights(p, g, raw_scores(p, g, top + d, active), active, masked)
                accumulate(p, g, top + d, a, active)
        for p, g in groups_here:
            a_scr[p, 1] = weights(p, g, raw_scores(p, g, top, everyone), everyone,
                                  {c for c in chains if c % halves == 0})

        @pl.when(gg < extra_chunks)
        def _():
            decode_chunk()

        assert halves == 2

        def key_body(m, _, p, g):
            nxt = top - 1 - halves * m
            chunk = decode_begin()
            decode_parts = [decode_values, decode_keys]
            for cur in range(halves):
                accumulate(p, g, nxt - cur + 1, a_scr[p, 1 - cur], everyone)
                z_scr[p, 1 - cur] = raw_scores(p, g, jnp.maximum(nxt - cur - 1, 0),
                                               everyone)
                a_scr[p, cur] = weights(p, g, z_scr[p, cur], everyone, set())
                decode_parts[cur](chunk)
            decode_end(chunk)
            return 0

        for p, g in groups_here:
            lax.fori_loop(0, i, functools.partial(key_body, p=p, g=g), 0)
        for p, g in groups_here:
            accumulate(p, g, 0, a_scr[p, 1], everyone)
            base = p * len(chains)
            heads = [jnp.concatenate([acc_scr[base + hh * halves + half]
                                      for half in range(halves)], axis=0)
                     for hh in range(HEADS_PER_GROUP)]
            lane_q = lax.broadcasted_iota(jnp.int32, heads[0].shape, 1)
            o_scr[g] = jnp.where(lane_q < HEAD_DIM, heads[0], heads[1])
        return 0

    lax.fori_loop(0, N_GROUPS // GROUPS_PER_BODY, group_body, 0)

    o = jnp.concatenate([o_scr[g] for g in range(N_GROUPS)], axis=1)
    y_ref[0] = _mixer_out(o, sga_ref[0], ya_ref[0], x_ref[0],
                          _split_mod(mod_ref[0])[2], gattn_ref[...], wout_ref,
                          gpost_ref[...])

    @pl.when(last_step)
    def _():
        def drain(_, carry):
            decode_chunk()
            return carry

        lax.fori_loop(chunk_ref[0], n_chunks, drain, 0)
        decode_values(n_chunks)
        finish_sequence(n_chunks)


def _attention(page_table, b_sb, q4, k4, v4, tri2, ya, sga, x, mod, g_attn, w_out_b,
               g_post, qcol, bias_col, cache_k, cache_v):
    B, S, D = x.shape
    n_seq, n_pages = page_table.shape
    page_tokens = cache_k.shape[2]
    assert n_pages % PAGES_PER_CHUNK == 0 and KEY_SEGMENT % page_tokens == 0
    blk = ATT_QUERY_TILE
    n_chain = HEADS_PER_GROUP * (blk // ATT_BLOCK)
    cdim = ya.shape[-1]
    n_tiles = S // blk
    n_chunks = n_seq * (n_pages // PAGES_PER_CHUNK)
    bodies = N_GROUPS // GROUPS_PER_BODY
    trips = B * N_GROUPS * sum(range(n_tiles))
    assert trips <= n_chunks, "more key-loop trips than decode chunks"
    extra_chunks = min(bodies, (n_chunks - trips) // (B * n_tiles))
    row_blk = lambda w: pl.BlockSpec((1, blk, w), lambda b, i, pt: (b, i, 0))
    once = pl.Buffered(1)
    const = lambda shp: pl.BlockSpec(shp, lambda b, i, pt: (0,) * len(shp),
                                     pipeline_mode=once)
    whole_seq = pl.BlockSpec((1, N_GROUPS, S, LANES), lambda b, i, pt: (b, 0, 0, 0),
                             pipeline_mode=once)
    page_buf = lambda slots: pltpu.VMEM(
        (slots, PAGES_PER_CHUNK, ATT_DIM, page_tokens), F32)
    grid_spec = pltpu.PrefetchScalarGridSpec(
        num_scalar_prefetch=1,
        grid=(B, S // blk),
        in_specs=[pl.BlockSpec(memory_space=pltpu.SMEM),
                  pl.BlockSpec((1, N_GROUPS, blk, LANES), lambda b, i, pt: (b, 0, i, 0)),
                  whole_seq, whole_seq, const(tri2.shape),
                  row_blk(cdim), row_blk(ATT_DIM), row_blk(D),
                  pl.BlockSpec((1, 1, 3 * D), lambda b, i, pt: (b, 0, 0)),
                  const((1, ATT_DIM)), const(w_out_b.shape), const((1, D)),
                  pl.BlockSpec(memory_space=pl.ANY),
                  const(bias_col.shape),
                  pl.BlockSpec(memory_space=pl.ANY),
                  pl.BlockSpec(memory_space=pl.ANY)],
        out_specs=[row_blk(D),
                   pl.BlockSpec((n_seq, ATT_DIM), lambda b, i, pt: (0, 0))],
        scratch_shapes=[pltpu.VMEM((GROUPS_PER_BODY * n_chain, ATT_BLOCK, LANES), BF16),
                        pltpu.VMEM((GROUPS_PER_BODY, 2, n_chain * ATT_BLOCK, ATT_BLOCK), F32),
                        pltpu.VMEM((GROUPS_PER_BODY, 2, n_chain * ATT_BLOCK, ATT_BLOCK), BF16),
                        pltpu.VMEM((GROUPS_PER_BODY * n_chain, ATT_BLOCK, LANES), F32),
                        pltpu.VMEM((GROUPS_PER_BODY * n_chain, ATT_BLOCK, LANES), F32),
                        pltpu.VMEM((N_GROUPS, blk, LANES), F32),
                        page_buf(CHUNKS_AHEAD + 1), page_buf(CHUNKS_AHEAD + 2),
                        pltpu.VMEM((2, ATT_DIM, LANES), F32),
                        pltpu.SemaphoreType.DMA((CHUNKS_AHEAD + 1,)),
                        pltpu.SemaphoreType.DMA((CHUNKS_AHEAD + 2,)),
                        pltpu.SemaphoreType.DMA((2,)),
                        pltpu.VMEM((N_HEADS, PAGES_PER_CHUNK * page_tokens), F32),
                        pltpu.VMEM((ATT_DIM, LANES), F32),
                        pltpu.VMEM((N_HEADS, LANES), F32),
                        pltpu.SMEM((1,), jnp.int32)],
    )
    return pl.pallas_call(
        functools.partial(_attn_kernel, extra_chunks=extra_chunks),
        grid_spec=grid_spec,
        out_shape=[jax.ShapeDtypeStruct((B, S, D), F32),
                   jax.ShapeDtypeStruct((n_seq, ATT_DIM), F32)],
        compiler_params=pltpu.CompilerParams(
            dimension_semantics=("arbitrary", "arbitrary"),
            vmem_limit_bytes=VMEM_LIMIT),
        name="attention",
    )(page_table, b_sb, q4, k4, v4, tri2, ya, sga, x, mod, g_attn, w_out_b, g_post,
      qcol, bias_col, cache_k, cache_v)


def _sample_proj_kernel(x_ref, mod_ref, st_ref, gpre_ref, w_ref, cw_ref,
                        gconv_ref, ya_ref, q_ref, k_ref, v_ref, ga_ref, conv_ref):
    cdim = cw_ref.shape[1]
    h = _modulated_norm(x_ref[...], mod_ref[...], gpre_ref[...])
    p = jnp.dot(h, w_ref[...], preferred_element_type=F32)
    seg = lambda i: p[:, i * cdim:(i + 1) * cdim]
    u = seg(2) * seg(0)
    u2, u1 = st_ref[:, 0:cdim], st_ref[:, cdim:2 * cdim]
    ya_ref[...] = _conv_branch(u, u1, u2, seg(1), seg(3), cw_ref[...],
                               gconv_ref[...]).astype(BF16)
    conv_ref[:, 0:cdim] = u1
    conv_ref[:, cdim:2 * cdim] = u
    q_ref[...] = seg(4) * (HEAD_DIM ** -0.5)
    k_ref[...] = seg(5)
    v_ref[...] = seg(6)
    ga = seg(7)
    ga_ref[...] = ga * jax.nn.sigmoid(ga)


def _sample_proj(x, mod, state, g_pre, w_in_b, conv_w, g_conv):
    n, _ = x.shape
    cdim = conv_w.shape[1]
    sds = lambda w, dt=F32: jax.ShapeDtypeStruct((n, w), dt)
    return pl.pallas_call(
        _sample_proj_kernel,
        out_shape=[sds(cdim, BF16), sds(ATT_DIM), sds(ATT_DIM), sds(ATT_DIM),
                   sds(ATT_DIM), sds((CONV_WIDTH - 1) * cdim)],
        compiler_params=pltpu.CompilerParams(vmem_limit_bytes=VMEM_LIMIT),
        name="sample_proj",
    )(x, mod, state, g_pre, w_in_b, conv_w, g_conv)


def _sample_out_kernel(o_ref, sga_ref, ya_ref, x_ref, mod_ref, gattn_ref,
                       wout_ref, gpost_ref, y_ref):
    y_ref[...] = _mixer_out(o_ref[...], sga_ref[...], ya_ref[...], x_ref[...],
                            _split_mod(mod_ref[...])[2], gattn_ref[...], wout_ref,
                            gpost_ref[...])


def _sample_out(o, sga, ya, x, mod, g_attn, w_out_b, g_post):
    return pl.pallas_call(
        _sample_out_kernel,
        out_shape=jax.ShapeDtypeStruct(x.shape, F32),
        compiler_params=pltpu.CompilerParams(vmem_limit_bytes=VMEM_LIMIT),
        name="sample_out",
    )(o, sga, ya, x, mod, g_attn, w_out_b, g_post)


def _layer(xp, xs, c_all, pool_k, pool_v, state, page_table, w_ada, b_ada, g_pre,
           w_in, conv_w, g_conv, b_sb, g_attn, w_out, g_post):
    B, S, D = xp.shape
    n_seq = xs.shape[0]
    row = lambda v: v.reshape(1, -1)
    w_in_b = w_in.astype(BF16)
    w_out_b = w_out.astype(BF16)

    mod = _ada(c_all, w_ada, row(b_ada))
    mod_p, mod_s = mod[:B].reshape(B, 1, 3 * D), mod[B:B + n_seq]

    ya, q4, k, kb4, v, vb4, sga, conv_p = _prompt_proj(
        xp, mod_p, row(g_pre), w_in_b, conv_w, row(g_conv))
    xs2 = xs.reshape(n_seq, D)
    ya_s, q_s, k_s, v_s, sga_s, conv_s = _sample_proj(
        xs2, mod_s, state.reshape(n_seq, -1), row(g_pre), w_in_b, conv_w,
        row(g_conv))
    conv_s = conv_s.reshape(state.shape)

    idx = jnp.arange(ATT_BLOCK)
    tri = (idx[:, None] >= idx[None, :]).astype(BF16)
    tri2 = jnp.concatenate([tri, tri], axis=0)
    n_pool, page_tokens = pool_k.shape[:2]
    assert page_tokens == LANES
    pages = lambda pool: jnp.transpose(pool, (0, 2, 3, 1)).reshape(
        n_pool, ATT_DIM, page_tokens)
    qcol = jnp.broadcast_to(q_s[:, :, None], (n_seq, ATT_DIM, LANES))
    bias_col = jnp.broadcast_to(b_sb[:, None], (N_HEADS, LANES))
    yp, o_s = _attention(page_table, b_sb, q4, kb4, vb4, tri2, ya, sga, xp, mod_p,
                         row(g_attn), w_out_b, row(g_post), qcol, bias_col,
                         pages(pool_k), pages(pool_v))
    ys = _sample_out(o_s, sga_s, ya_s, xs2, mod_s, row(g_attn), w_out_b, row(g_post))

    heads = lambda t, n, L: t.reshape(n, L, N_HEADS, HEAD_DIM)
    return (yp, ys.reshape(xs.shape), heads(k, B, S), heads(v, B, S), conv_p,
            heads(k_s, n_seq, 1), heads(v_s, n_seq, 1), conv_s)


def kernel(x_prompt, x_sample, c_prompt, c_sample, cache_k, cache_v, state_conv,
           page_table, w_ada, b_ada, g_pre, w_in, conv_w, g_conv, b_sb, g_attn,
           w_out, g_post):
    assert x_sample.shape[1] == 1, "decode step handles one new token per sequence"
    depth = w_in.shape[0]
    n_rows = c_prompt.shape[0] + c_sample.shape[0]
    pad = (-n_rows) % 8
    c_all = jnp.concatenate(
        [c_prompt, c_sample, jnp.zeros((pad, c_prompt.shape[1]), F32)], axis=0)
    xp, xs = x_prompt, x_sample
    outs = [[] for _ in range(6)]
    for l in range(depth):
        xp, xs, kp, vp, cp, ksn, vsn, csn = _layer(
            xp, xs, c_all, cache_k[l], cache_v[l], state_conv[l], page_table,
            w_ada[l], b_ada[l], g_pre[l], w_in[l], conv_w[l], g_conv[l], b_sb[l],
            g_attn[l], w_out[l], g_post[l])
        for lst, t in zip(outs, (kp, vp, cp, ksn, vsn, csn)):
            lst.append(t)
    return (xp, xs) + tuple(jnp.stack(t) for t in outs)
```

```python
import functools

import jax
import jax.numpy as jnp
from jax import lax
from jax.experimental import pallas as pl
from jax.experimental.pallas import tpu as pltpu

N_HEADS = 8
HEAD_DIM = 64
ATT_DIM = N_HEADS * HEAD_DIM
CONV_WIDTH = 3
EPS = 1e-6

LANES = 128
HEADS_PER_GROUP = LANES // HEAD_DIM
N_GROUPS = ATT_DIM // LANES

PROJ_ROWS = 512
ATT_BLOCK = 256
ATT_QUERY_TILE = 2 * ATT_BLOCK
GROUPS_PER_BODY = 2
PAGES_PER_CHUNK = 8
CHUNKS_AHEAD = 2
KEY_SEGMENT = ATT_BLOCK
VMEM_LIMIT = 56 * 1024 * 1024

F32 = jnp.float32
BF16 = jnp.bfloat16
LOG2E = 1.4426950408889634
SOFTPLUS2_LINEAR_ABOVE = 64.0


def _rms(x, g):
    ms = jnp.mean(x * x, axis=-1, keepdims=True)
    return x * lax.rsqrt(ms + EPS) * g


def _softplus(z):
    return jnp.maximum(z, 0.0) + jnp.log(1.0 + jnp.exp(-jnp.abs(z)))


def _softplus2(z2):
    return jnp.where(z2 > SOFTPLUS2_LINEAR_ABOVE, z2,
                     jnp.log2(1.0 + jnp.exp2(z2)))


def _split_bf16(x):
    hi = x.astype(BF16)
    lo = (x - hi.astype(F32)).astype(BF16)
    return hi, lo


def _suffix_sum(sp, tri2):
    hi, lo = _split_bf16(sp)
    return jnp.dot(jnp.concatenate([hi, lo], axis=1), tri2,
                   preferred_element_type=F32)


def _ada_kernel(c_ref, w_ref, b_ref, o_ref):
    c = c_ref[...]
    a = c * jax.nn.sigmoid(c)
    o_ref[...] = jnp.dot(a, w_ref[...], preferred_element_type=F32,
                         precision=lax.Precision.HIGHEST) + b_ref[...]


def _ada(c_all, w_ada, b_ada):
    rows = c_all.shape[0]
    return pl.pallas_call(
        _ada_kernel,
        out_shape=jax.ShapeDtypeStruct((rows, w_ada.shape[1]), F32),
        compiler_params=pltpu.CompilerParams(vmem_limit_bytes=VMEM_LIMIT),
        name="ada",
    )(c_all, w_ada, b_ada)


def _split_mod(mod):
    d = mod.shape[-1] // 3
    return mod[:, 0:d], mod[:, d:2 * d], mod[:, 2 * d:3 * d]


def _modulated_norm(x, mod, g_pre):
    shift, scale, _ = _split_mod(mod)
    return (_rms(x, g_pre) * (1.0 + scale) + shift).astype(BF16)


def _conv_branch(u, u1, u2, bc, gc, conv_w, g_conv):
    conv_y = conv_w[0:1] * u2 + conv_w[1:2] * u1 + conv_w[2:3] * u
    return _rms(bc * conv_y, g_conv) * (gc * jax.nn.sigmoid(gc))


def _prompt_proj_kernel(x_ref, mod_ref, gpre_ref, w_ref, cw_ref, gconv_ref,
                        ya_ref, q_ref, k_ref, kb_ref, v_ref, vb_ref, ga_ref,
                        conv_ref, u_scr):
    rows = x_ref.shape[1]
    cdim = cw_ref.shape[1]
    h = _modulated_norm(x_ref[0], mod_ref[0], gpre_ref[...])

    def seg(i):
        return jnp.dot(h, w_ref[:, i * cdim:(i + 1) * cdim],
                       preferred_element_type=F32)

    @pl.when(pl.program_id(1) == 0)
    def _():
        u_scr[0:8, :] = jnp.zeros((8, cdim), F32)

    u = seg(2) * seg(0)
    u_scr[8:8 + rows, :] = u
    u1 = u_scr[7:7 + rows, :]
    u2 = u_scr[6:6 + rows, :]
    ya = _conv_branch(u, u1, u2, seg(1), seg(3), cw_ref[...], gconv_ref[...])
    ya_ref[0] = ya.astype(BF16)
    tail = u_scr[rows + 6:rows + 8, :]
    conv_ref[0] = tail
    u_scr[6:8, :] = tail

    q = (seg(4) * (LOG2E * HEAD_DIM ** -0.5)).astype(BF16)
    k = seg(5)
    v = seg(6)
    k_ref[0] = k
    v_ref[0] = v
    kb = k.astype(BF16)
    vb = v.astype(BF16)
    for g in range(N_GROUPS):
        sl = slice(g * LANES, (g + 1) * LANES)
        q_ref[0, g] = q[:, sl]
        kb_ref[0, g] = kb[:, sl]
        vb_ref[0, g] = vb[:, sl]
    ga = seg(7)
    ga_ref[0] = ga * jax.nn.sigmoid(ga)


def _prompt_proj(x, mod, g_pre, w_in_b, conv_w, g_conv):
    B, S, D = x.shape
    cdim = conv_w.shape[1]
    rows = PROJ_ROWS
    grid = (B, S // rows)
    row_blk = lambda w: pl.BlockSpec((1, rows, w), lambda b, s: (b, s, 0))
    grp_blk = pl.BlockSpec((1, N_GROUPS, rows, LANES), lambda b, s: (b, 0, s, 0))
    const = lambda shp: pl.BlockSpec(shp, lambda b, s: (0,) * len(shp))
    grp_shape = jax.ShapeDtypeStruct((B, N_GROUPS, S, LANES), BF16)
    return pl.pallas_call(
        _prompt_proj_kernel,
        grid=grid,
        in_specs=[row_blk(D),
                  pl.BlockSpec((1, 1, 3 * D), lambda b, s: (b, 0, 0)),
                  const((1, D)), const(w_in_b.shape), const(conv_w.shape),
                  const((1, cdim))],
        out_specs=[row_blk(cdim), grp_blk, row_blk(ATT_DIM), grp_blk,
                   row_blk(ATT_DIM), grp_blk, row_blk(ATT_DIM),
                   pl.BlockSpec((1, CONV_WIDTH - 1, cdim), lambda b, s: (b, 0, 0))],
        out_shape=[jax.ShapeDtypeStruct((B, S, cdim), BF16), grp_shape,
                   jax.ShapeDtypeStruct((B, S, ATT_DIM), F32), grp_shape,
                   jax.ShapeDtypeStruct((B, S, ATT_DIM), F32), grp_shape,
                   jax.ShapeDtypeStruct((B, S, ATT_DIM), F32),
                   jax.ShapeDtypeStruct((B, CONV_WIDTH - 1, cdim), F32)],
        scratch_shapes=[pltpu.VMEM((rows + 8, cdim), F32)],
        compiler_params=pltpu.CompilerParams(
            dimension_semantics=("arbitrary", "arbitrary"),
            vmem_limit_bytes=VMEM_LIMIT),
        name="prompt_proj",
    )(x, mod, g_pre, w_in_b, conv_w, g_conv)


def _mixer_out(o, sga, ya, x, gate, g_attn, w_out_ref, g_post):
    cdim = ya.shape[-1]
    yb = (_rms(o, g_attn) * sga).astype(BF16)
    m = (jnp.dot(ya, w_out_ref[0:cdim, :], preferred_element_type=F32)
         + jnp.dot(yb, w_out_ref[cdim:, :], preferred_element_type=F32))
    return x + gate * _rms(m, g_post)


def _decode_keys(kpages, qcol_ref, bias, tri2, carry_ref, fresh):
    n_pages, _, page_tokens = kpages.shape
    seg_pages = KEY_SEGMENT // page_tokens
    n_seg = n_pages // seg_pages
    sub = 8

    z_rows = [[None] * N_HEADS for _ in range(n_pages)]
    for h in range(N_HEADS):
        part = [None] * n_pages
        for r in range(HEAD_DIM // sub):
            rows = slice(h * HEAD_DIM + r * sub, h * HEAD_DIM + (r + 1) * sub)
            q = qcol_ref[rows, :]
            for p in range(n_pages):
                t = kpages[p, rows, :] * q
                part[p] = t if r == 0 else part[p] + t
        for p in range(n_pages):
            z_rows[p][h] = jnp.sum(part[p], axis=0, keepdims=True)
    z_seg, sp_seg = [], []
    for s in range(n_seg):
        z = jnp.concatenate(
            [jnp.concatenate(z_rows[p], axis=0)
             for p in range(s * seg_pages, (s + 1) * seg_pages)], axis=1)
        z = z + jnp.concatenate([bias] * seg_pages, axis=1)
        z_seg.append(z)
        sp_seg.append(_softplus(z))

    suffix = _suffix_sum(jnp.concatenate(sp_seg, axis=0), tri2)

    carry = jnp.where(fresh, 0.0, carry_ref[...])
    a_seg = [None] * n_seg
    for s in reversed(range(n_seg)):
        c = jnp.concatenate([carry] * seg_pages, axis=1)
        a_seg[s] = jnp.exp(z_seg[s] - suffix[s * N_HEADS:(s + 1) * N_HEADS] - c)
        total = jnp.sum(sp_seg[s], axis=1, keepdims=True)
        carry = carry + jnp.broadcast_to(total, carry.shape)
    carry_ref[...] = carry
    return jnp.concatenate(a_seg, axis=1)


def _decode_values(vpages, a_ref, acc_ref, fresh):
    n_pages, _, page_tokens = vpages.shape
    for h in range(N_HEADS):
        rows = slice(h * HEAD_DIM, (h + 1) * HEAD_DIM)
        acc = jnp.where(fresh, 0.0, acc_ref[rows, :])
        for p in range(n_pages):
            a_row = a_ref[h:h + 1, p * page_tokens:(p + 1) * page_tokens]
            acc = acc + vpages[p, rows, :] * a_row
        acc_ref[rows, :] = acc


def _attn_kernel(pt_ref, b_ref, q_ref, k_ref, v_ref, tri_ref,
                 qcol_hbm, bcol_ref, ck_hbm, cv_hbm,
                 y_ref, os_ref,
                 qm_scr, z_scr, a_scr, acc_scr, carry_scr, o_scr,
                 kbuf, vbuf, qbuf, ksem, vsem, qsem, da_scr, dacc_scr, dcarry_scr,
                 chunk_ref, *,
                 extra_chunks):
    blk = ATT_BLOCK
    halves = q_ref.shape[2] // blk
    chains = range(HEADS_PER_GROUP * halves)
    i = pl.program_id(1)
    lane = lax.broadcasted_iota(jnp.int32, (blk, LANES), 1)
    row = lax.broadcasted_iota(jnp.int32, (blk, blk), 0)
    col = lax.broadcasted_iota(jnp.int32, (blk, blk), 1)
    causal = col < row
    tri2 = tri_ref[...]

    n_seq, n_pages = pt_ref.shape
    groups = n_pages // PAGES_PER_CHUNK
    n_chunks = n_seq * groups

    def chunk_copies(c):
        kslot = c % kbuf.shape[0]
        vslot = c % vbuf.shape[0]
        seq = c // groups
        first = (groups - 1 - c % groups) * PAGES_PER_CHUNK
        out = []
        for p in range(PAGES_PER_CHUNK):
            page = pt_ref[seq, first + p]
            out.append(pltpu.make_async_copy(ck_hbm.at[page], kbuf.at[kslot, p],
                                             ksem.at[kslot]))
            out.append(pltpu.make_async_copy(cv_hbm.at[page], vbuf.at[vslot, p],
                                             vsem.at[vslot]))
        return out

    def query_copy(c):
        seq = c // groups
        return pltpu.make_async_copy(qcol_hbm.at[seq], qbuf.at[seq % 2],
                                     qsem.at[seq % 2])

    def start_chunk(c):
        for cp in chunk_copies(c):
            cp.start()

        @pl.when(c % groups == 0)
        def _():
            query_copy(c).start()

    def wait_chunk(c):
        for cp in chunk_copies(c):
            cp.wait()

        @pl.when(c % groups == 0)
        def _():
            query_copy(c).wait()

    def decode_begin():
        c = chunk_ref[0]

        @pl.when(c + CHUNKS_AHEAD < n_chunks)
        def _():
            start_chunk(c + CHUNKS_AHEAD)

        wait_chunk(c)
        return c

    def decode_values(c):
        _decode_values(vbuf.at[(c + vbuf.shape[0] - 1) % vbuf.shape[0]], da_scr,
                       dacc_scr, (c + groups - 1) % groups == 0)

    def finish_sequence(c):
        @pl.when((c % groups == 0) & (c > 0))
        def _():
            os_ref[pl.ds((c - 1) // groups, 1), :] = jnp.sum(dacc_scr[...].T, axis=0,
                                                             keepdims=True)

    def decode_keys(c):
        da_scr[...] = _decode_keys(kbuf.at[c % kbuf.shape[0]], qbuf.at[(c // groups) % 2],
                                   bcol_ref[...], tri2, dcarry_scr, c % groups == 0)

    def decode_compute(c):
        decode_values(c)
        decode_keys(c)

    def decode_end(c):
        finish_sequence(c)
        chunk_ref[0] = c + 1

    def decode_chunk():
        c = decode_begin()
        decode_compute(c)
        decode_end(c)

    first_step = (pl.program_id(0) == 0) & (i == 0)
    last_step = ((pl.program_id(0) == pl.num_programs(0) - 1)
                 & (i == pl.num_programs(1) - 1))

    @pl.when(first_step)
    def _():
        chunk_ref[0] = 0
        dacc_scr[...] = jnp.zeros_like(dacc_scr)
        dcarry_scr[...] = jnp.zeros_like(dcarry_scr)
        da_scr[...] = jnp.zeros_like(da_scr)
        vbuf[vbuf.shape[0] - 1] = jnp.zeros(vbuf.shape[1:], F32)
        for c in range(CHUNKS_AHEAD):
            start_chunk(jnp.int32(c))

    def raw_scores(p, g, key_blk, active):
        off = pl.multiple_of(key_blk * blk, blk)
        kb = k_ref[0, g, pl.ds(off, blk), :]
        qs = jnp.concatenate([qm_scr[p * len(chains) + c] for c in active], axis=0)
        return lax.dot_general(qs, kb, (((1,), (1,)), ((), ())),
                               preferred_element_type=F32)

    def weights(p, g, s, active, masked):
        base = p * len(chains)
        z, sp = [], []
        for n, c in enumerate(active):
            bias2 = b_ref[g * HEADS_PER_GROUP + c // halves] * LOG2E
            zc = s[n * blk:(n + 1) * blk] + bias2
            spc = _softplus2(zc)
            if c in masked:
                spc = jnp.where(causal, spc, 0.0)
            z.append(zc)
            sp.append(spc)
        suffix = jnp.dot(jnp.concatenate(sp, axis=0).astype(BF16), tri2[0:blk],
                         preferred_element_type=F32)
        a = []
        for n, c in enumerate(active):
            carry = carry_scr[base + c]
            c2 = jnp.concatenate([carry] * (blk // LANES), axis=1)
            ac = jnp.exp2(z[n] - suffix[n * blk:(n + 1) * blk] - c2)
            if c in masked:
                ac = jnp.where(causal, ac, 0.0)
            a.append(ac.astype(BF16))
            total = jnp.sum(sp[n], axis=1, keepdims=True)
            carry_scr[base + c] = carry + jnp.broadcast_to(total, (blk, LANES))
        return jnp.concatenate(a, axis=0)

    def accumulate(p, g, key_blk, a, active):
        base = p * len(chains)
        off = pl.multiple_of(key_blk * blk, blk)
        vb = v_ref[0, g, pl.ds(off, blk), :]
        o = jnp.dot(a, vb, preferred_element_type=F32)
        for n, c in enumerate(active):
            acc_scr[base + c] += o[n * blk:(n + 1) * blk]

    def group_body(gg, _):
        groups_here = [(p, gg * GROUPS_PER_BODY + p) for p in range(GROUPS_PER_BODY)]
        for p, g in groups_here:
            qg = q_ref[0, g]
            for c in chains:
                hh, half = divmod(c, halves)
                in_head = (lane >= hh * HEAD_DIM) & (lane < (hh + 1) * HEAD_DIM)
                qh = qg[half * blk:(half + 1) * blk]
                qm_scr[p * len(chains) + c] = jnp.where(in_head, qh, jnp.zeros_like(qh))
        acc_scr[...] = jnp.zeros_like(acc_scr)
        carry_scr[...] = jnp.zeros_like(carry_scr)

        top = halves * i
        everyone = list(chains)
        for p, g in groups_here:
            z_scr[p, 0] = raw_scores(p, g, jnp.maximum(top - 1, 0), everyone)

        for d in reversed(range(1, halves)):
            active = [c for c in chains if c % halves >= d]
            masked = {c for c in active if c % halves == d}
            for p, g in groups_here:
                a = weights(p, g, raw_scores(p, g, top + d, active), active, masked)
                accumulate(p, g, top + d, a, active)
        for p, g in groups_here:
            a_scr[p, 1] = weights(p, g, raw_scores(p, g, top, everyone), everyone,
                                  {c for c in chains if c % halves == 0})

        @pl.when(gg < extra_chunks)
        def _():
            decode_chunk()

        assert halves == 2

        def key_body(m, _, p, g):
            nxt = top - 1 - halves * m
            chunk = decode_begin()
            decode_parts = [decode_values, decode_keys]
            for cur in range(halves):
                accumulate(p, g, nxt - cur + 1, a_scr[p, 1 - cur], everyone)
                z_scr[p, 1 - cur] = raw_scores(p, g, jnp.maximum(nxt - cur - 1, 0),
                                               everyone)
                a_scr[p, cur] = weights(p, g, z_scr[p, cur], everyone, set())
                decode_parts[cur](chunk)
            decode_end(chunk)
            return 0

        for p, g in groups_here:
            lax.fori_loop(0, i, functools.partial(key_body, p=p, g=g), 0)
        for p, g in groups_here:
            accumulate(p, g, 0, a_scr[p, 1], everyone)
            base = p * len(chains)
            heads = [jnp.concatenate([acc_scr[base + hh * halves + half]
                                      for half in range(halves)], axis=0)
                     for hh in range(HEADS_PER_GROUP)]
            lane_q = lax.broadcasted_iota(jnp.int32, heads[0].shape, 1)
            o_scr[g] = jnp.where(lane_q < HEAD_DIM, heads[0], heads[1])
        return 0

    lax.fori_loop(0, N_GROUPS // GROUPS_PER_BODY, group_body, 0)

    y_ref[0] = jnp.concatenate([o_scr[g] for g in range(N_GROUPS)], axis=1)

    @pl.when(last_step)
    def _():
        def drain(_, carry):
            decode_chunk()
            return carry

        lax.fori_loop(chunk_ref[0], n_chunks, drain, 0)
        decode_values(n_chunks)
        finish_sequence(n_chunks)


def _attention(page_table, b_sb, q4, k4, v4, tri2, qcol, bias_col, cache_k, cache_v):
    B, _, S, _ = q4.shape
    n_seq, n_pages = page_table.shape
    page_tokens = cache_k.shape[2]
    assert n_pages % PAGES_PER_CHUNK == 0 and KEY_SEGMENT % page_tokens == 0
    blk = ATT_QUERY_TILE
    n_chain = HEADS_PER_GROUP * (blk // ATT_BLOCK)
    n_tiles = S // blk
    n_chunks = n_seq * (n_pages // PAGES_PER_CHUNK)
    bodies = N_GROUPS // GROUPS_PER_BODY
    trips = B * N_GROUPS * sum(range(n_tiles))
    assert trips <= n_chunks, "more key-loop trips than decode chunks"
    extra_chunks = min(bodies, (n_chunks - trips) // (B * n_tiles))
    row_blk = lambda w: pl.BlockSpec((1, blk, w), lambda b, i, pt: (b, i, 0))
    once = pl.Buffered(1)
    const = lambda shp: pl.BlockSpec(shp, lambda b, i, pt: (0,) * len(shp),
                                     pipeline_mode=once)
    whole_seq = pl.BlockSpec((1, N_GROUPS, S, LANES), lambda b, i, pt: (b, 0, 0, 0))
    page_buf = lambda slots: pltpu.VMEM(
        (slots, PAGES_PER_CHUNK, ATT_DIM, page_tokens), F32)
    grid_spec = pltpu.PrefetchScalarGridSpec(
        num_scalar_prefetch=1,
        grid=(B, S // blk),
        in_specs=[pl.BlockSpec(memory_space=pltpu.SMEM),
                  pl.BlockSpec((1, N_GROUPS, blk, LANES), lambda b, i, pt: (b, 0, i, 0)),
                  whole_seq, whole_seq, const(tri2.shape),
                  pl.BlockSpec(memory_space=pl.ANY),
                  const(bias_col.shape),
                  pl.BlockSpec(memory_space=pl.ANY),
                  pl.BlockSpec(memory_space=pl.ANY)],
        out_specs=[row_blk(ATT_DIM),
                   pl.BlockSpec((n_seq, ATT_DIM), lambda b, i, pt: (0, 0))],
        scratch_shapes=[pltpu.VMEM((GROUPS_PER_BODY * n_chain, ATT_BLOCK, LANES), BF16),
                        pltpu.VMEM((GROUPS_PER_BODY, 2, n_chain * ATT_BLOCK, ATT_BLOCK), F32),
                        pltpu.VMEM((GROUPS_PER_BODY, 2, n_chain * ATT_BLOCK, ATT_BLOCK), BF16),
                        pltpu.VMEM((GROUPS_PER_BODY * n_chain, ATT_BLOCK, LANES), F32),
                        pltpu.VMEM((GROUPS_PER_BODY * n_chain, ATT_BLOCK, LANES), F32),
                        pltpu.VMEM((N_GROUPS, blk, LANES), F32),
                        page_buf(CHUNKS_AHEAD + 1), page_buf(CHUNKS_AHEAD + 2),
                        pltpu.VMEM((2, ATT_DIM, LANES), F32),
                        pltpu.SemaphoreType.DMA((CHUNKS_AHEAD + 1,)),
                        pltpu.SemaphoreType.DMA((CHUNKS_AHEAD + 2,)),
                        pltpu.SemaphoreType.DMA((2,)),
                        pltpu.VMEM((N_HEADS, PAGES_PER_CHUNK * page_tokens), F32),
                        pltpu.VMEM((ATT_DIM, LANES), F32),
                        pltpu.VMEM((N_HEADS, LANES), F32),
                        pltpu.SMEM((1,), jnp.int32)],
    )
    return pl.pallas_call(
        functools.partial(_attn_kernel, extra_chunks=extra_chunks),
        grid_spec=grid_spec,
        out_shape=[jax.ShapeDtypeStruct((B, S, ATT_DIM), F32),
                   jax.ShapeDtypeStruct((n_seq, ATT_DIM), F32)],
        compiler_params=pltpu.CompilerParams(
            dimension_semantics=("arbitrary", "arbitrary"),
            vmem_limit_bytes=VMEM_LIMIT),
        name="attention",
    )(page_table, b_sb, q4, k4, v4, tri2, qcol, bias_col, cache_k, cache_v)


def _prompt_out_kernel(o_ref, sga_ref, ya_ref, x_ref, mod_ref, gattn_ref, wout_ref,
                       gpost_ref, y_ref):
    y_ref[0] = _mixer_out(o_ref[0], sga_ref[0], ya_ref[0], x_ref[0],
                          _split_mod(mod_ref[0])[2], gattn_ref[...], wout_ref,
                          gpost_ref[...])


def _prompt_out(o, sga, ya, x, mod, g_attn, w_out_b, g_post):
    B, S, D = x.shape
    rows = PROJ_ROWS
    row_blk = lambda w: pl.BlockSpec((1, rows, w), lambda b, s: (b, s, 0))
    const = lambda shp: pl.BlockSpec(shp, lambda b, s: (0,) * len(shp))
    return pl.pallas_call(
        _prompt_out_kernel,
        grid=(B, S // rows),
        in_specs=[row_blk(ATT_DIM), row_blk(ATT_DIM), row_blk(ya.shape[-1]), row_blk(D),
                  pl.BlockSpec((1, 1, 3 * D), lambda b, s: (b, 0, 0)),
                  const((1, ATT_DIM)), const(w_out_b.shape), const((1, D))],
        out_specs=row_blk(D),
        out_shape=jax.ShapeDtypeStruct((B, S, D), F32),
        compiler_params=pltpu.CompilerParams(
            dimension_semantics=("arbitrary", "arbitrary"),
            vmem_limit_bytes=VMEM_LIMIT),
        name="prompt_out",
    )(o, sga, ya, x, mod, g_attn, w_out_b, g_post)


def _sample_proj_kernel(x_ref, mod_ref, st_ref, gpre_ref, w_ref, cw_ref,
                        gconv_ref, ya_ref, q_ref, k_ref, v_ref, ga_ref, conv_ref):
    cdim = cw_ref.shape[1]
    h = _modulated_norm(x_ref[...], mod_ref[...], gpre_ref[...])
    p = jnp.dot(h, w_ref[...], preferred_element_type=F32)
    seg = lambda i: p[:, i * cdim:(i + 1) * cdim]
    u = seg(2) * seg(0)
    u2, u1 = st_ref[:, 0:cdim], st_ref[:, cdim:2 * cdim]
    ya_ref[...] = _conv_branch(u, u1, u2, seg(1), seg(3), cw_ref[...],
                               gconv_ref[...]).astype(BF16)
    conv_ref[:, 0:cdim] = u1
    conv_ref[:, cdim:2 * cdim] = u
    q_ref[...] = seg(4) * (HEAD_DIM ** -0.5)
    k_ref[...] = seg(5)
    v_ref[...] = seg(6)
    ga = seg(7)
    ga_ref[...] = ga * jax.nn.sigmoid(ga)


def _sample_proj(x, mod, state, g_pre, w_in_b, conv_w, g_conv):
    n, _ = x.shape
    cdim = conv_w.shape[1]
    sds = lambda w, dt=F32: jax.ShapeDtypeStruct((n, w), dt)
    return pl.pallas_call(
        _sample_proj_kernel,
        out_shape=[sds(cdim, BF16), sds(ATT_DIM), sds(ATT_DIM), sds(ATT_DIM),
                   sds(ATT_DIM), sds((CONV_WIDTH - 1) * cdim)],
        compiler_params=pltpu.CompilerParams(vmem_limit_bytes=VMEM_LIMIT),
        name="sample_proj",
    )(x, mod, state, g_pre, w_in_b, conv_w, g_conv)


def _sample_out_kernel(o_ref, sga_ref, ya_ref, x_ref, mod_ref, gattn_ref,
                       wout_ref, gpost_ref, y_ref):
    y_ref[...] = _mixer_out(o_ref[...], sga_ref[...], ya_ref[...], x_ref[...],
                            _split_mod(mod_ref[...])[2], gattn_ref[...], wout_ref,
                            gpost_ref[...])


def _sample_out(o, sga, ya, x, mod, g_attn, w_out_b, g_post):
    return pl.pallas_call(
        _sample_out_kernel,
        out_shape=jax.ShapeDtypeStruct(x.shape, F32),
        compiler_params=pltpu.CompilerParams(vmem_limit_bytes=VMEM_LIMIT),
        name="sample_out",
    )(o, sga, ya, x, mod, g_attn, w_out_b, g_post)


def _layer(xp, xs, c_all, pool_k, pool_v, state, page_table, w_ada, b_ada, g_pre,
           w_in, conv_w, g_conv, b_sb, g_attn, w_out, g_post):
    B, S, D = xp.shape
    n_seq = xs.shape[0]
    row = lambda v: v.reshape(1, -1)
    w_in_b = w_in.astype(BF16)
    w_out_b = w_out.astype(BF16)

    mod = _ada(c_all, w_ada, row(b_ada))
    mod_p, mod_s = mod[:B].reshape(B, 1, 3 * D), mod[B:B + n_seq]

    ya, q4, k, kb4, v, vb4, sga, conv_p = _prompt_proj(
        xp, mod_p, row(g_pre), w_in_b, conv_w, row(g_conv))
    xs2 = xs.reshape(n_seq, D)
    ya_s, q_s, k_s, v_s, sga_s, conv_s = _sample_proj(
        xs2, mod_s, state.reshape(n_seq, -1), row(g_pre), w_in_b, conv_w,
        row(g_conv))
    conv_s = conv_s.reshape(state.shape)

    idx = jnp.arange(ATT_BLOCK)
    tri = (idx[:, None] >= idx[None, :]).astype(BF16)
    tri2 = jnp.concatenate([tri, tri], axis=0)
    n_pool, page_tokens = pool_k.shape[:2]
    assert page_tokens == LANES
    pages = lambda pool: jnp.transpose(pool, (0, 2, 3, 1)).reshape(
        n_pool, ATT_DIM, page_tokens)
    qcol = jnp.broadcast_to(q_s[:, :, None], (n_seq, ATT_DIM, LANES))
    bias_col = jnp.broadcast_to(b_sb[:, None], (N_HEADS, LANES))
    o_p, o_s = _attention(page_table, b_sb, q4, kb4, vb4, tri2, qcol, bias_col,
                          pages(pool_k), pages(pool_v))
    yp = _prompt_out(o_p, sga, ya, xp, mod_p, row(g_attn), w_out_b, row(g_post))
    ys = _sample_out(o_s, sga_s, ya_s, xs2, mod_s, row(g_attn), w_out_b, row(g_post))

    heads = lambda t, n, L: t.reshape(n, L, N_HEADS, HEAD_DIM)
    return (yp, ys.reshape(xs.shape), heads(k, B, S), heads(v, B, S), conv_p,
            heads(k_s, n_seq, 1), heads(v_s, n_seq, 1), conv_s)


def kernel(x_prompt, x_sample, c_prompt, c_sample, cache_k, cache_v, state_conv,
           page_table, w_ada, b_ada, g_pre, w_in, conv_w, g_conv, b_sb, g_attn,
           w_out, g_post):
    assert x_sample.shape[1] == 1, "decode step handles one new token per sequence"
    depth = w_in.shape[0]
    n_rows = c_prompt.shape[0] + c_sample.shape[0]
    pad = (-n_rows) % 8
    c_all = jnp.concatenate(
        [c_prompt, c_sample, jnp.zeros((pad, c_prompt.shape[1]), F32)], axis=0)
    xp, xs = x_prompt, x_sample
    outs = [[] for _ in range(6)]
    for l in range(depth):
        xp, xs, kp, vp, cp, ksn, vsn, csn = _layer(
            xp, xs, c_all, cache_k[l], cache_v[l], state_conv[l], page_table,
            w_ada[l], b_ada[l], g_pre[l], w_in[l], conv_w[l], g_conv[l], b_sb[l],
            g_attn[l], w_out[l], g_post[l])
        for lst, t in zip(outs, (kp, vp, cp, ksn, vsn, csn)):
            lst.append(t)
    return (xp, xs) + tuple(jnp.stack(t) for t in outs)
```
